```python
import jax, jax.numpy as jnp
from jax import lax
import numpy as np

D_MODEL = 1024
BATCH = 2
SEQ = 8192
DEPTH = 4
DEC_BATCH = 128
DEC_SEQ = 8
PAST_LEN = 8192
PAGE_SIZE = 128

N_MIXERS = 3
N_MLA_LAYERS = (DEPTH + 2) // 3
N_FOX_LAYERS = (DEPTH + 1) // 3
N_CONV_LAYERS = DEPTH // 3

MLA_HEADS = 16
MLA_Q_LORA = 384
MLA_KV_LORA = 256
MLA_NOPE = 64
MLA_ROPE = 32
MLA_V = 64
MLA_QK = MLA_NOPE + MLA_ROPE
ROPE_BASE = 10000.0

FOX_HEADS = 16
FOX_KV_HEADS = 4
FOX_HEAD_DIM = 64
FOX_GROUP = FOX_HEADS // FOX_KV_HEADS
FORGET_BIAS = 3.0

CONV_CH = D_MODEL
CONV_WIDTH = 31

D_FF = 4 * D_MODEL
Q_BLOCK = 128
NORM_EPS = 1e-6
NEG_INF = -1e30

kernel_name = "hybrid_mla_fox_conformer_decode_step"


def rms_norm(x, g):
    xf = x.astype(jnp.float32)
    y = xf * lax.rsqrt(jnp.mean(xf * xf, axis=-1, keepdims=True) + NORM_EPS)
    return (y * g.astype(jnp.float32)).astype(x.dtype)


def layer_norm(x, g, b):
    xf = x.astype(jnp.float32)
    mu = jnp.mean(xf, axis=-1, keepdims=True)
    var = jnp.mean(jnp.square(xf - mu), axis=-1, keepdims=True)
    y = (xf - mu) * lax.rsqrt(var + NORM_EPS)
    return (y * g.astype(jnp.float32) + b.astype(jnp.float32)).astype(x.dtype)


def rope(x, pos):
    half = x.shape[-1] // 2
    inv = ROPE_BASE ** (-jnp.arange(half, dtype=jnp.float32) / half)
    ang = pos.astype(jnp.float32)[:, None] * inv[None, :]
    cos = jnp.cos(ang)[:, None, :]
    sin = jnp.sin(ang)[:, None, :]
    x1 = x[..., :half].astype(jnp.float32)
    x2 = x[..., half:].astype(jnp.float32)
    return jnp.concatenate([x1 * cos - x2 * sin, x2 * cos + x1 * sin], axis=-1).astype(x.dtype)


def sq_relu_mlp(h, w_up, w_down):
    a = jax.nn.relu(h @ w_up)
    return (a * a) @ w_down


def mla_queries(h, w_dq, g_qa, w_uq, g_qn, pos):
    cq = rms_norm(h @ w_dq, g_qa)
    q = jnp.einsum('...c,chd->...hd', cq, w_uq)
    q = rms_norm(q, g_qn)
    return jnp.concatenate([q[..., :MLA_NOPE], rope(q[..., MLA_NOPE:], pos)], axis=-1)


def mla_latent(h, w_dkv, g_kva):
    a = h @ w_dkv
    return rms_norm(a[..., :MLA_KV_LORA], g_kva), a[..., MLA_KV_LORA:]


def mla_keys(lat, kpe, pos, w_ukv, g_kn):
    kv = jnp.einsum('...c,chd->...hd', lat, w_ukv)
    k_nope, v = kv[..., :MLA_NOPE], kv[..., MLA_NOPE:]
    kpe_h = jnp.broadcast_to(kpe[..., None, :], k_nope.shape[:-1] + (MLA_ROPE,))
    k = rms_norm(jnp.concatenate([k_nope, kpe_h], axis=-1), g_kn)
    k = jnp.concatenate([k[..., :MLA_NOPE], rope(k[..., MLA_NOPE:], pos)], axis=-1)
    return k, v


def mla_prompt(h, pos, w_dq, g_qa, w_uq, g_qn, w_dkv, g_kva, w_ukv, g_kn, w_o):
    b, s, _ = h.shape
    q = mla_queries(h, w_dq, g_qa, w_uq, g_qn, pos)
    lat, kpe = mla_latent(h, w_dkv, g_kva)
    k, v = mla_keys(lat, kpe, pos, w_ukv, g_kn)
    scale = MLA_QK ** -0.5
    kpos = jnp.arange(s)

    def one_block(i):
        start = i * Q_BLOCK
        qb = lax.dynamic_slice_in_dim(q, start, Q_BLOCK, axis=1)
        sc = jnp.einsum('bqhd,bkhd->bhqk', qb, k).astype(jnp.float32) * scale
        qpos = start + jnp.arange(Q_BLOCK)
        sc = jnp.where(kpos[None, :] <= qpos[:, None], sc, NEG_INF)
        p = jax.nn.softmax(sc, axis=-1).astype(v.dtype)
        return jnp.einsum('bhqk,bkhd->bqhd', p, v)

    o = lax.map(one_block, jnp.arange(s // Q_BLOCK))
    o = jnp.moveaxis(o, 0, 1).reshape(b, s, MLA_HEADS, MLA_V)
    return jnp.einsum('bshd,hdm->bsm', o, w_o), lat, kpe


def mla_sample(h, pos_new, past_len, cache_lat, cache_kpe, layer, page_table,
               w_dq, g_qa, w_uq, g_qn, w_dkv, g_kva, w_ukv, g_kn, w_o):
    t = h.shape[1]
    q = mla_queries(h, w_dq, g_qa, w_uq, g_qn, pos_new)
    lat, kpe = mla_latent(h, w_dkv, g_kva)
    pos_all = jnp.arange(past_len + t)
    mask = pos_all[None, :] <= pos_new[:, None]
    scale = MLA_QK ** -0.5

    def one_seq(args):
        q1, lat1, kpe1, pages = args
        lat_all = jnp.concatenate([cache_lat[layer, pages].reshape(past_len, MLA_KV_LORA), lat1], axis=0)
        kpe_all = jnp.concatenate([cache_kpe[layer, pages].reshape(past_len, MLA_ROPE), kpe1], axis=0)
        k, v = mla_keys(lat_all, kpe_all, pos_all, w_ukv, g_kn)
        sc = jnp.einsum('thd,lhd->htl', q1, k).astype(jnp.float32) * scale
        sc = jnp.where(mask, sc, NEG_INF)
        p = jax.nn.softmax(sc, axis=-1).astype(v.dtype)
        return jnp.einsum('htl,lhd->thd', p, v)

    o = lax.map(one_seq, (q, lat, kpe, page_table))
    return jnp.einsum('nthd,hdm->ntm', o, w_o), lat, kpe


def fox_project(h, w_q, w_k, w_v, w_f, b_f, g_qn, g_kn):
    q = rms_norm(jnp.einsum('...m,mhd->...hd', h, w_q), g_qn)
    k = rms_norm(jnp.einsum('...m,mhd->...hd', h, w_k), g_kn)
    v = jnp.einsum('...m,mhd->...hd', h, w_v)
    lf = jax.nn.log_sigmoid((h @ w_f + b_f).astype(jnp.float32))
    return q, k, v, lf


def fox_prompt(h, w_q, w_k, w_v, w_f, b_f, g_qn, g_kn, w_o):
    b, s, _ = h.shape
    q, k, v, lf = fox_project(h, w_q, w_k, w_v, w_f, b_f, g_qn, g_kn)
    F = lax.cumsum(lf, axis=1)
    Fg = F.reshape(b, s, FOX_KV_HEADS, FOX_GROUP)
    Fk = jnp.transpose(Fg, (0, 2, 3, 1))[:, :, :, None, :]
    qg = q.reshape(b, s, FOX_KV_HEADS, FOX_GROUP, FOX_HEAD_DIM)
    scale = FOX_HEAD_DIM ** -0.5
    kpos = jnp.arange(s)

    def one_block(i):
        start = i * Q_BLOCK
        qb = lax.dynamic_slice_in_dim(qg, start, Q_BLOCK, axis=1)
        Fq = jnp.transpose(lax.dynamic_slice_in_dim(Fg, start, Q_BLOCK, axis=1), (0, 2, 3, 1))[..., None]
        sc = jnp.einsum('bqngd,bknd->bngqk', qb, k).astype(jnp.float32) * scale + (Fq - Fk)
        qpos = start + jnp.arange(Q_BLOCK)
        sc = jnp.where(kpos[None, :] <= qpos[:, None], sc, NEG_INF)
        p = jax.nn.softmax(sc, axis=-1).astype(v.dtype)
        return jnp.einsum('bngqk,bknd->bqngd', p, v)

    o = lax.map(one_block, jnp.arange(s // Q_BLOCK))
    o = jnp.moveaxis(o, 0, 1).reshape(b, s, FOX_HEADS, FOX_HEAD_DIM)
    return jnp.einsum('bshd,hdm->bsm', o, w_o), k, v, lf


def fox_sample(h, past_len, cache_k, cache_v, cache_lf, layer, page_table,
               w_q, w_k, w_v, w_f, b_f, g_qn, g_kn, w_o):
    t = h.shape[1]
    q, k, v, lf = fox_project(h, w_q, w_k, w_v, w_f, b_f, g_qn, g_kn)
    scale = FOX_HEAD_DIM ** -0.5
    causal_new = (jnp.arange(t)[:, None] >= jnp.arange(t)[None, :])[:, None, :]

    def one_seq(args):
        q1, k1, v1, lf1, pages = args
        kp = cache_k[layer, pages].reshape(past_len, FOX_KV_HEADS, FOX_HEAD_DIM)
        vp = cache_v[layer, pages].reshape(past_len, FOX_KV_HEADS, FOX_HEAD_DIM)
        lfp = cache_lf[layer, pages].reshape(past_len, FOX_HEADS).astype(jnp.float32)
        G = lax.cumsum(lfp, axis=0, reverse=True) - lfp
        Fn = lax.cumsum(lf1, axis=0)
        bias_past = Fn[:, :, None] + G.T[None, :, :]
        bias_new = jnp.where(causal_new, Fn[:, :, None] - Fn.T[None, :, :], NEG_INF)
        bias = jnp.concatenate([bias_past, bias_new], axis=-1)
        K = jnp.concatenate([kp, k1], axis=0)
        V = jnp.concatenate([vp, v1], axis=0)
        qg = q1.reshape(t, FOX_KV_HEADS, FOX_GROUP, FOX_HEAD_DIM)
        sc = jnp.einsum('tngd,lnd->tngl', qg, K).astype(jnp.float32) * scale
        sc = sc + bias.reshape(t, FOX_KV_HEADS, FOX_GROUP, -1)
        p = jax.nn.softmax(sc, axis=-1).astype(V.dtype)
        return jnp.einsum('tngl,lnd->tngd', p, V).reshape(t, FOX_HEADS, FOX_HEAD_DIM)

    o = lax.map(one_seq, (q, k, v, lf, page_table))
    return jnp.einsum('nthd,hdm->ntm', o, w_o), k, v, lf


def conv_pre(h, w_pw1, b_pw1):
    a = h @ w_pw1 + b_pw1
    return a[..., :CONV_CH] * jax.nn.sigmoid(a[..., CONV_CH:])


def conv_post(u_ext, w_dw, b_dw, g_ln, b_ln, w_pw2, b_pw2):
    z = lax.conv_general_dilated(u_ext, w_dw[:, None, :].astype(u_ext.dtype), (1,), 'VALID',
                                 dimension_numbers=('NWC', 'WIO', 'NWC'),
                                 feature_group_count=CONV_CH) + b_dw
    z = jax.nn.silu(layer_norm(z, g_ln, b_ln))
    return z @ w_pw2 + b_pw2


def setup_inputs(seed: int = 0) -> dict:
    key = jax.random.key(seed)
    ks = iter(jax.random.split(key, 48))

    def nrm(shape, scale=1.0):
        return jax.random.normal(next(ks), shape, jnp.float32) * scale

    def gain(shape):
        return 1.0 + nrm(shape, 0.02)

    n_pages = PAST_LEN // PAGE_SIZE
    used = DEC_BATCH * n_pages
    n_pool = used + used // 4 + 1
    d = D_MODEL
    inp = {}
    inp['x_prompt'] = nrm((BATCH, SEQ, d))
    inp['x_sample'] = nrm((DEC_BATCH, DEC_SEQ, d))
    inp['cache_mla_latent'] = nrm((N_MLA_LAYERS, n_pool, PAGE_SIZE, MLA_KV_LORA))
    inp['cache_mla_krope'] = nrm((N_MLA_LAYERS, n_pool, PAGE_SIZE, MLA_ROPE))
    inp['cache_fox_k'] = nrm((N_FOX_LAYERS, n_pool, PAGE_SIZE, FOX_KV_HEADS, FOX_HEAD_DIM))
    inp['cache_fox_v'] = nrm((N_FOX_LAYERS, n_pool, PAGE_SIZE, FOX_KV_HEADS, FOX_HEAD_DIM))
    inp['cache_fox_logf'] = jax.nn.log_sigmoid(FORGET_BIAS + nrm((N_FOX_LAYERS, n_pool, PAGE_SIZE, FOX_HEADS)))
    inp['state_conv'] = nrm((N_CONV_LAYERS, DEC_BATCH, CONV_WIDTH - 1, CONV_CH), 0.5)
    inp['page_table'] = jax.random.permutation(next(ks), n_pool)[:used].reshape(DEC_BATCH, n_pages).astype(jnp.int32)
    inp['norm_mix'] = gain((DEPTH, d))
    inp['norm_mlp'] = gain((DEPTH, d))
    inp['mlp_up'] = nrm((DEPTH, d, D_FF), d ** -0.5)
    inp['mlp_down'] = nrm((DEPTH, D_FF, d), D_FF ** -0.5)
    L = N_MLA_LAYERS
    inp['mla_w_dq'] = nrm((L, d, MLA_Q_LORA), d ** -0.5)
    inp['mla_g_qa'] = gain((L, MLA_Q_LORA))
    inp['mla_w_uq'] = nrm((L, MLA_Q_LORA, MLA_HEADS, MLA_QK), MLA_Q_LORA ** -0.5)
    inp['mla_g_qn'] = gain((L, MLA_QK))
    inp['mla_w_dkv'] = nrm((L, d, MLA_KV_LORA + MLA_ROPE), d ** -0.5)
    inp['mla_g_kva'] = gain((L, MLA_KV_LORA))
    inp['mla_w_ukv'] = nrm((L, MLA_KV_LORA, MLA_HEADS, MLA_NOPE + MLA_V), MLA_KV_LORA ** -0.5)
    inp['mla_g_kn'] = gain((L, MLA_QK))
    inp['mla_w_o'] = nrm((L, MLA_HEADS, MLA_V, d), (MLA_HEADS * MLA_V) ** -0.5)
    L = N_FOX_LAYERS
    inp['fox_w_q'] = nrm((L, d, FOX_HEADS, FOX_HEAD_DIM), d ** -0.5)
    inp['fox_w_k'] = nrm((L, d, FOX_KV_HEADS, FOX_HEAD_DIM), d ** -0.5)
    inp['fox_w_v'] = nrm((L, d, FOX_KV_HEADS, FOX_HEAD_DIM), d ** -0.5)
    inp['fox_w_f'] = nrm((L, d, FOX_HEADS), d ** -0.5)
    inp['fox_b_f'] = FORGET_BIAS + nrm((L, FOX_HEADS), 0.1)
    inp['fox_g_qn'] = gain((L, FOX_HEAD_DIM))
    inp['fox_g_kn'] = gain((L, FOX_HEAD_DIM))
    inp['fox_w_o'] = nrm((L, FOX_HEADS, FOX_HEAD_DIM, d), (FOX_HEADS * FOX_HEAD_DIM) ** -0.5)
    L = N_CONV_LAYERS
    inp['conv_w_pw1'] = nrm((L, d, 2 * CONV_CH), d ** -0.5)
    inp['conv_b_pw1'] = nrm((L, 2 * CONV_CH), 0.01)
    inp['conv_w_dw'] = nrm((L, CONV_WIDTH, CONV_CH), CONV_WIDTH ** -0.5)
    inp['conv_b_dw'] = nrm((L, CONV_CH), 0.01)
    inp['conv_g_ln'] = gain((L, CONV_CH))
    inp['conv_b_ln'] = nrm((L, CONV_CH), 0.01)
    inp['conv_w_pw2'] = nrm((L, CONV_CH, d), CONV_CH ** -0.5)
    inp['conv_b_pw2'] = nrm((L, d), 0.01)
    return inp


def reference(x_prompt, x_sample, cache_mla_latent, cache_mla_krope, cache_fox_k, cache_fox_v,
              cache_fox_logf, state_conv, page_table, norm_mix, norm_mlp, mlp_up, mlp_down,
              mla_w_dq, mla_g_qa, mla_w_uq, mla_g_qn, mla_w_dkv, mla_g_kva, mla_w_ukv, mla_g_kn, mla_w_o,
              fox_w_q, fox_w_k, fox_w_v, fox_w_f, fox_b_f, fox_g_qn, fox_g_kn, fox_w_o,
              conv_w_pw1, conv_b_pw1, conv_w_dw, conv_b_dw, conv_g_ln, conv_b_ln, conv_w_pw2, conv_b_pw2):
    seq = x_prompt.shape[1]
    dec_seq = x_sample.shape[1]
    past_len = page_table.shape[1] * PAGE_SIZE
    pos_prompt = jnp.arange(seq)
    pos_new = past_len + jnp.arange(dec_seq)

    xp, xs = x_prompt, x_sample
    mla_lat_p, mla_lat_s, mla_kpe_p, mla_kpe_s = [], [], [], []
    fox_k_p, fox_k_s, fox_v_p, fox_v_s, fox_lf_p, fox_lf_s = [], [], [], [], [], []
    conv_p, conv_s = [], []

    for i in range(DEPTH):
        kind, j = i % N_MIXERS, i // N_MIXERS
        hp = rms_norm(xp, norm_mix[i])
        hs = rms_norm(xs, norm_mix[i])
        if kind == 0:
            w = (mla_w_dq[j], mla_g_qa[j], mla_w_uq[j], mla_g_qn[j], mla_w_dkv[j], mla_g_kva[j],
                 mla_w_ukv[j], mla_g_kn[j], mla_w_o[j])
            yp, lat_p, kpe_p = mla_prompt(hp, pos_prompt, *w)
            ys, lat_s, kpe_s = mla_sample(hs, pos_new, past_len, cache_mla_latent, cache_mla_krope,
                                          j, page_table, *w)
            mla_lat_p.append(lat_p); mla_kpe_p.append(kpe_p)
            mla_lat_s.append(lat_s); mla_kpe_s.append(kpe_s)
        elif kind == 1:
            w = (fox_w_q[j], fox_w_k[j], fox_w_v[j], fox_w_f[j], fox_b_f[j], fox_g_qn[j],
                 fox_g_kn[j], fox_w_o[j])
            yp, k_p, v_p, lf_p = fox_prompt(hp, *w)
            ys, k_s, v_s, lf_s = fox_sample(hs, past_len, cache_fox_k, cache_fox_v, cache_fox_logf,
                                            j, page_table, *w)
            fox_k_p.append(k_p); fox_v_p.append(v_p); fox_lf_p.append(lf_p)
            fox_k_s.append(k_s); fox_v_s.append(v_s); fox_lf_s.append(lf_s)
        else:
            post = (conv_w_dw[j], conv_b_dw[j], conv_g_ln[j], conv_b_ln[j], conv_w_pw2[j], conv_b_pw2[j])
            u_p = conv_pre(hp, conv_w_pw1[j], conv_b_pw1[j])
            u_p_ext = jnp.pad(u_p, ((0, 0), (CONV_WIDTH - 1, 0), (0, 0)))
            yp = conv_post(u_p_ext, *post)
            u_s = conv_pre(hs, conv_w_pw1[j], conv_b_pw1[j])
            u_s_ext = jnp.concatenate([state_conv[j].astype(u_s.dtype), u_s], axis=1)
            ys = conv_post(u_s_ext, *post)
            conv_p.append(u_p_ext[:, -(CONV_WIDTH - 1):])
            conv_s.append(u_s_ext[:, -(CONV_WIDTH - 1):])
        xp = xp + yp
        xs = xs + ys
        xp = xp + sq_relu_mlp(rms_norm(xp, norm_mlp[i]), mlp_up[i], mlp_down[i])
        xs = xs + sq_relu_mlp(rms_norm(xs, norm_mlp[i]), mlp_up[i], mlp_down[i])

    y_prompt, y_sample = xp, xs
    new_mla_latent_prompt = jnp.stack(mla_lat_p)
    new_mla_latent_sample = jnp.stack(mla_lat_s)
    new_mla_krope_prompt = jnp.stack(mla_kpe_p)
    new_mla_krope_sample = jnp.stack(mla_kpe_s)
    new_fox_k_prompt = jnp.stack(fox_k_p)
    new_fox_k_sample = jnp.stack(fox_k_s)
    new_fox_v_prompt = jnp.stack(fox_v_p)
    new_fox_v_sample = jnp.stack(fox_v_s)
    new_fox_logf_prompt = jnp.stack(fox_lf_p)
    new_fox_logf_sample = jnp.stack(fox_lf_s)
    new_conv_state_prompt = jnp.stack(conv_p)
    new_conv_state_sample = jnp.stack(conv_s)
    return (y_prompt, y_sample, new_mla_latent_prompt, new_mla_latent_sample,
            new_mla_krope_prompt, new_mla_krope_sample, new_fox_k_prompt, new_fox_k_sample,
            new_fox_v_prompt, new_fox_v_sample, new_fox_logf_prompt, new_fox_logf_sample,
            new_conv_state_prompt, new_conv_state_sample)
```

```python
import functools
import math

import jax
import jax.numpy as jnp
from jax import lax
from jax.experimental import pallas as pl
from jax.experimental.pallas import tpu as pltpu

F32 = jnp.float32
BF16 = jnp.bfloat16

NORM_EPS = 1e-6
NEG_INF = -1e30
ROPE_BASE = 10000.0
FORGET_SPLIT = 3
LANES = 128
PAGE_SIZE = 128
PAGES_PER_STEP = 8
VMEM_LIMIT = 56 * 1024 * 1024

NT_DIMS = (((1,), (1,)), ((), ()))


def _params(n_axes):
    return pltpu.CompilerParams(dimension_semantics=("arbitrary",) * n_axes, vmem_limit_bytes=VMEM_LIMIT)


def _tile(n, cands):
    for c in cands:
        if n % c == 0:
            return c
    raise ValueError(f"no tile for {n}")


def _dot(a, b):
    return jnp.dot(a, b, preferred_element_type=F32)


def _dot_nt(a, b):
    return lax.dot_general(a, b, NT_DIMS, preferred_element_type=F32)


def _rms(x, g, n=None):
    n = x.shape[-1] if n is None else n
    ss = jnp.sum(x * x, axis=-1, keepdims=True)
    return x * lax.rsqrt(ss * (1.0 / n) + NORM_EPS) * g


def _split(x, pieces=FORGET_SPLIT):
    out = []
    for _ in range(pieces):
        p = x.astype(BF16)
        out.append(p)
        x = x - p.astype(F32)
    return out


def _full(shape):
    nd = len(shape)
    return pl.BlockSpec(shape, lambda *_: (0,) * nd)


def _mlp_body(x_ref, g_ref, wu_ref, wd_ref, o_ref, h_sc, acc_sc):
    f = pl.program_id(1)

    @pl.when(f == 0)
    def _():
        h_sc[...] = _rms(x_ref[...], g_ref[...]).astype(BF16)
        acc_sc[...] = jnp.zeros_like(acc_sc)

    a = jnp.maximum(_dot(h_sc[...], wu_ref[...]), 0.0)
    acc_sc[...] += _dot((a * a).astype(BF16), wd_ref[...])

    @pl.when(f == pl.num_programs(1) - 1)
    def _():
        o_ref[...] = x_ref[...] + acc_sc[...]


def _mlp(x, g, w_up, w_down):
    r, d = x.shape
    dff = w_up.shape[1]
    tm = _tile(r, (1024, 512, 256, 128, 64, 32, 16, 8))
    tf = _tile(dff, (512, 256, 128))
    return pl.pallas_call(
        _mlp_body,
        grid=(r // tm, dff // tf),
        in_specs=[pl.BlockSpec((tm, d), lambda i, f: (i, 0)),
                  pl.BlockSpec((1, d), lambda i, f: (0, 0)),
                  pl.BlockSpec((d, tf), lambda i, f: (0, f)),
                  pl.BlockSpec((tf, d), lambda i, f: (f, 0))],
        out_specs=pl.BlockSpec((tm, d), lambda i, f: (i, 0)),
        out_shape=jax.ShapeDtypeStruct((r, d), F32),
        scratch_shapes=[pltpu.VMEM((tm, d), BF16), pltpu.VMEM((tm, d), F32)],
        compiler_params=_params(2),
        name="mlp",
    )(x, g.reshape(1, d), w_up.astype(BF16), w_down.astype(BF16))


def _proj_res_body(x_ref, a_ref, w_ref, b_ref, o_ref):
    o_ref[...] = x_ref[...] + _dot(a_ref[...], w_ref[...]) + b_ref[...]


def _proj_res(x, a, w, b):
    r, d = x.shape
    k = a.shape[1]
    tm = _tile(r, (512, 256, 128, 64, 32, 16, 8))
    return pl.pallas_call(
        _proj_res_body,
        grid=(r // tm,),
        in_specs=[pl.BlockSpec((tm, d), lambda i: (i, 0)),
                  pl.BlockSpec((tm, k), lambda i: (i, 0)),
                  _full((k, d)), _full((1, d))],
        out_specs=pl.BlockSpec((tm, d), lambda i: (i, 0)),
        out_shape=jax.ShapeDtypeStruct((r, d), F32),
        compiler_params=_params(1),
        name="proj_res",
    )(x, a, w.astype(BF16), b.reshape(1, d).astype(F32))


def _flash_body(q_ref, k_ref, v_ref, o_ref, m_sc, l_sc, acc_sc, *, t):
    qi = pl.program_id(2)
    q = q_ref[...]
    m_sc[...] = jnp.full_like(m_sc, NEG_INF)
    l_sc[...] = jnp.zeros_like(l_sc)
    acc_sc[...] = jnp.zeros_like(acc_sc)

    def step(j, masked):
        start = pl.multiple_of(j * t, t)
        s = _dot_nt(q, k_ref[pl.ds(start, t), :])
        if masked:
            row = lax.broadcasted_iota(jnp.int32, (t, t), 0)
            col = lax.broadcasted_iota(jnp.int32, (t, t), 1)
            s = jnp.where(col <= row, s, NEG_INF)
        m_prev = m_sc[...]
        m_new = jnp.maximum(m_prev, jnp.max(s, axis=-1, keepdims=True))
        alpha = jnp.exp(m_prev - m_new)
        p = jnp.exp(s - m_new)
        l_sc[...] = alpha * l_sc[...] + jnp.sum(p, axis=-1, keepdims=True)
        acc_sc[...] = alpha * acc_sc[...] + _dot(p.astype(BF16), v_ref[pl.ds(start, t), :])
        m_sc[...] = m_new

    def unmasked(j, carry):
        step(j, False)
        return carry

    lax.fori_loop(0, qi, unmasked, 0)
    step(qi, True)
    o_ref[...] = (acc_sc[...] / l_sc[...]).astype(o_ref.dtype)


def _flash(q, k, v, batch, seq, heads, group):
    t = _tile(seq, (512, 256, 128, 64, 32, 16))
    nq = seq // t
    return pl.pallas_call(
        functools.partial(_flash_body, t=t),
        grid=(batch, heads, nq),
        in_specs=[pl.BlockSpec((t, LANES), lambda b, h, i: (b * nq + i, h)),
                  pl.BlockSpec((seq, LANES), lambda b, h, i: (b, h // group)),
                  pl.BlockSpec((seq, LANES), lambda b, h, i: (b, h // group))],
        out_specs=pl.BlockSpec((t, LANES), lambda b, h, i: (b * nq + i, h)),
        out_shape=jax.ShapeDtypeStruct((batch * seq, heads * LANES), BF16),
        scratch_shapes=[pltpu.VMEM((t, 1), F32), pltpu.VMEM((t, 1), F32), pltpu.VMEM((t, LANES), F32)],
        compiler_params=_params(3),
        name="flash",
    )(q, k, v)


def _rope_slot(x, cos, sin, lane):
    swapped = jnp.where(lane < 80, pltpu.roll(x, 112, 1), pltpu.roll(x, 16, 1))
    return x * cos + swapped * sin


def _mla_pre_body(x_ref, gmix_ref, wdq_ref, gqa_ref, wuq_ref, gq_ref, wdkv_ref, gkva_ref, wuk_ref, wuv_ref,
                  gk_ref, cos_ref, sin_ref, q_out, k_out, v_out, lat_out, kpe_out, *, heads, qk_dim):
    h = _rms(x_ref[...], gmix_ref[...]).astype(BF16)
    cq = _rms(_dot(h, wdq_ref[...]), gqa_ref[...]).astype(BF16)
    a = _dot(h, wdkv_ref[...])
    kv_lora = gkva_ref.shape[1]
    lat = _rms(a[:, :kv_lora], gkva_ref[...])
    lat_out[...] = lat
    kpe = a[:, kv_lora:]
    kpe_out[...] = kpe
    latb = lat.astype(BF16)
    cos, sin = cos_ref[...], sin_ref[...]
    lane = lax.broadcasted_iota(jnp.int32, (1, LANES), 1)
    scale = qk_dim ** -0.5
    for p in range(heads // 2):
        cols = slice(p * 2 * LANES, (p + 1) * 2 * LANES)
        q2 = _dot(cq, wuq_ref[:, cols])
        k2 = _dot(latb, wuk_ref[:, cols])
        v_out[:, cols] = _dot(latb, wuv_ref[:, cols]).astype(BF16)
        for e in range(2):
            sl = slice(e * LANES, (e + 1) * LANES)
            out = slice((2 * p + e) * LANES, (2 * p + e + 1) * LANES)
            qh = _rms(q2[:, sl], gq_ref[...], qk_dim) * scale
            q_out[:, out] = _rope_slot(qh, cos, sin, lane).astype(BF16)
            kh = _rms(k2[:, sl] + kpe, gk_ref[...], qk_dim)
            k_out[:, out] = _rope_slot(kh, cos, sin, lane).astype(BF16)


def _mla_pre(x, gmix, w, cos, sin, heads, qk_dim):
    r, d = x.shape
    tm = _tile(r, (512, 256, 128, 64, 32, 16, 8))
    hw = heads * LANES
    row = lambda c: pl.BlockSpec((tm, c), lambda i: (i, 0))
    ins = [x, gmix, w["w_dq"], w["g_qa"], w["w_uq"], w["g_q"], w["w_dkv"], w["g_kva"], w["w_uk"], w["w_uv"], w["g_k"]]
    return pl.pallas_call(
        functools.partial(_mla_pre_body, heads=heads, qk_dim=qk_dim),
        grid=(r // tm,),
        in_specs=[row(d)] + [_full(a.shape) for a in ins[1:]] + [row(LANES), row(LANES)],
        out_specs=[row(hw), row(hw), row(hw), row(w["g_kva"].shape[1]), row(LANES)],
        out_shape=[jax.ShapeDtypeStruct((r, hw), BF16)] * 3
        + [jax.ShapeDtypeStruct((r, w["g_kva"].shape[1]), F32), jax.ShapeDtypeStruct((r, LANES), F32)],
        compiler_params=_params(1),
        name="mla_pre",
    )(*ins, cos, sin)


def _mla_qdec_body(q_ref, gk_ref, w_ref, o_ref):
    o_ref[...] = _dot((q_ref[...].astype(F32) * gk_ref[...]).astype(BF16), w_ref[...])


def _mla_qdec(q_s, gk, wcat):
    nt = q_s.shape[0]
    heads, _, wd = wcat.shape
    return pl.pallas_call(
        _mla_qdec_body,
        grid=(heads,),
        in_specs=[pl.BlockSpec((nt, LANES), lambda h: (0, h)), _full((1, LANES)),
                  pl.BlockSpec((None, LANES, wd), lambda h: (h, 0, 0))],
        out_specs=pl.BlockSpec((None, nt, wd), lambda h: (h, 0, 0)),
        out_shape=jax.ShapeDtypeStruct((heads, nt, wd), F32),
        compiler_params=_params(1),
        name="mla_qdec",
    )(q_s, gk, wcat)


def _mla_unabsorb_body(o_ref, w_ref, out_ref):
    n, t, c = o_ref.shape
    out_ref[...] = _dot(o_ref[...].reshape(n * t, c).astype(BF16), w_ref[...]).astype(BF16)


def _mla_unabsorb(o_lat, w_uv):
    n, heads, t, c = o_lat.shape
    return pl.pallas_call(
        _mla_unabsorb_body,
        grid=(heads,),
        in_specs=[pl.BlockSpec((n, None, t, c), lambda h: (0, h, 0, 0)),
                  pl.BlockSpec((None, c, LANES), lambda h: (h, 0, 0))],
        out_specs=pl.BlockSpec((n * t, LANES), lambda h: (0, h)),
        out_shape=jax.ShapeDtypeStruct((n * t, heads * LANES), BF16),
        compiler_params=_params(1),
        name="mla_unabsorb",
    )(o_lat, w_uv)


def _softmax_update(s, v, m_sc, l_sc, acc_sc):
    m_prev = m_sc[...]
    m_new = jnp.maximum(m_prev, jnp.max(s, axis=-1, keepdims=True))
    alpha = jnp.exp(m_prev - m_new)
    p = jnp.exp(s - m_new)
    l_sc[...] = alpha * l_sc[...] + jnp.sum(p, axis=-1, keepdims=True)
    acc_sc[...] = alpha * acc_sc[...] + _dot(p.astype(BF16), v)
    m_sc[...] = m_new


def _mla_dec_body(pt_ref, q_ref, wukt_ref, gkr_ref, eye_ref, cos_ref, sin_ref, cosn_ref, sinn_ref, latn_ref, kpen_ref,
                  *rest, heads, t_new, qk_dim, nope):
    pps = PAGES_PER_STEP
    lat_refs, kpe_refs = rest[:pps], rest[pps:2 * pps]
    o_ref, m_sc, l_sc, acc_sc, rinv_sc, s_sc = rest[2 * pps:]
    c = pl.program_id(1)
    kv_lora = wukt_ref.shape[1]
    rows = heads * t_new

    @pl.when(c == 0)
    def _():
        m_sc[...] = jnp.full_like(m_sc, NEG_INF)
        l_sc[...] = jnp.zeros_like(l_sc)
        acc_sc[...] = jnp.zeros_like(acc_sc)

    q = q_ref[...].reshape(rows, q_ref.shape[-1]).astype(BF16)
    q_lat, q_rope = q[:, :kv_lora], q[:, kv_lora:kv_lora + LANES]
    rope_dim = qk_dim - nope
    half = rope_dim // 2

    def chunk(lat, kpe, cos, sin, new):
        n_keys = lat.shape[0]
        kt = _dot_nt(wukt_ref[...], lat)
        ss = jnp.sum((kt * kt).reshape(heads, nope, n_keys), axis=1)
        kpt = _dot_nt(eye_ref[...], kpe)
        ss = ss + jnp.sum(kpt * kpt, axis=0, keepdims=True)
        rinv_sc[:, :n_keys] = lax.rsqrt(ss * (1.0 / qk_dim) + NORM_EPS)
        kg = kpt * gkr_ref[...]
        x1, x2 = kg[:half], kg[half:]
        krot = jnp.concatenate([x1 * cos - x2 * sin, x2 * cos + x1 * sin], axis=0)
        krot = jnp.concatenate([krot, jnp.zeros((LANES - rope_dim, n_keys), F32)], axis=0).astype(BF16)
        s = _dot_nt(q_lat, lat) + _dot(q_rope, krot)
        for h in range(heads):
            sl = slice(h * t_new, (h + 1) * t_new)
            s_sc[sl, :n_keys] = s[sl] * rinv_sc[h:h + 1, :n_keys]
        s = s_sc[:, :n_keys]
        if new:
            tq = lax.broadcasted_iota(jnp.int32, (rows, n_keys), 0) % t_new
            tk = lax.broadcasted_iota(jnp.int32, (rows, n_keys), 1)
            s = jnp.where(tk <= tq, s, NEG_INF)
        _softmax_update(s, lat, m_sc, l_sc, acc_sc)

    lat = jnp.concatenate([r[...] for r in lat_refs], axis=0).astype(BF16)
    kpe = jnp.concatenate([r[...] for r in kpe_refs], axis=0).astype(BF16)
    chunk(lat, kpe, cos_ref[...], sin_ref[...], False)

    @pl.when(c == pl.num_programs(1) - 1)
    def _():
        chunk(latn_ref[...], kpen_ref[...], cosn_ref[...], sinn_ref[...], True)
        o_ref[...] = acc_sc[...] / l_sc[...]


def _mla_dec(page_table, qdec, wukt, gkr, eye, cos_t, sin_t, cos_n, sin_n, lat_new, kpe_new, cache_lat, cache_kpe,
             layer, heads, qk_dim, nope):
    n, n_pages = page_table.shape
    t_new = qdec.shape[2]
    kv_lora = wukt.shape[1]
    rope_dim = qk_dim - nope
    pps = PAGES_PER_STEP
    assert n_pages % pps == 0 and t_new <= PAGE_SIZE
    nc = n_pages // pps
    rows = heads * t_new
    chunk_keys = pps * PAGE_SIZE

    def page_spec(width, i):
        return pl.BlockSpec((None, None, PAGE_SIZE, width),
                            lambda s, c, pt: (layer, pt[s * n_pages + c * pps + i], 0, 0))

    const = lambda shape: pl.BlockSpec(shape, lambda s, c, pt: (0,) * len(shape))
    in_specs = [pl.BlockSpec((heads, None, t_new, qdec.shape[3]), lambda s, c, pt: (0, s, 0, 0)),
                const(wukt.shape), const(gkr.shape), const(eye.shape),
                pl.BlockSpec((rope_dim // 2, chunk_keys), lambda s, c, pt: (0, c)),
                pl.BlockSpec((rope_dim // 2, chunk_keys), lambda s, c, pt: (0, c)),
                const(cos_n.shape), const(sin_n.shape),
                pl.BlockSpec((None, PAGE_SIZE, kv_lora), lambda s, c, pt: (s, 0, 0)),
                pl.BlockSpec((None, PAGE_SIZE, rope_dim), lambda s, c, pt: (s, 0, 0))]
    in_specs += [page_spec(kv_lora, i) for i in range(pps)] + [page_spec(rope_dim, i) for i in range(pps)]
    grid_spec = pltpu.PrefetchScalarGridSpec(
        num_scalar_prefetch=1, grid=(n, nc), in_specs=in_specs,
        out_specs=pl.BlockSpec((None, rows, kv_lora), lambda s, c, pt: (s, 0, 0)),
        scratch_shapes=[pltpu.VMEM((rows, 1), F32), pltpu.VMEM((rows, 1), F32), pltpu.VMEM((rows, kv_lora), F32),
                        pltpu.VMEM((heads, chunk_keys), F32), pltpu.VMEM((rows, chunk_keys), F32)])
    return pl.pallas_call(
        functools.partial(_mla_dec_body, heads=heads, t_new=t_new, qk_dim=qk_dim, nope=nope),
        grid_spec=grid_spec,
        out_shape=jax.ShapeDtypeStruct((n, rows, kv_lora), F32),
        compiler_params=_params(2),
        name="mla_dec",
    )(page_table.reshape(-1), qdec, wukt, gkr, eye, cos_t, sin_t, cos_n, sin_n, lat_new, kpe_new,
      *([cache_lat] * pps), *([cache_kpe] * pps))


def _log_sigmoid(z):
    return jnp.minimum(z, 0.0) - jnp.log1p(jnp.exp(-jnp.abs(z)))


def _fox_pre_body(x_ref, gmix_ref, wq_ref, wk_ref, wv_ref, wf_ref, bf_ref, gq_ref, gk_ref, tri_ref, pq_ref, pk_ref,
                  cq_ref, ck_ref, qa_out, ka_out, vb_out, kf_out, vf_out, lf_out, carry_sc,
                  *, heads, kv_heads, head_dim, tiles_per_seq):
    i = pl.program_id(0)

    @pl.when(i % tiles_per_seq == 0)
    def _():
        carry_sc[...] = jnp.zeros_like(carry_sc)

    h = _rms(x_ref[...], gmix_ref[...]).astype(BF16)
    tm = h.shape[0]
    lane = lax.broadcasted_iota(jnp.int32, (1, LANES), 1)
    lf = jnp.where(lane < heads, _log_sigmoid(_dot(h, wf_ref[...]) + bf_ref[...]), 0.0)
    lf_out[...] = lf
    cum = carry_sc[...]
    for piece in _split(lf):
        cum = cum + _dot(tri_ref[...], piece)
    carry_sc[...] = cum[tm - 1:tm, :]
    cum_pieces = _split(cum)

    v = _dot(h, wv_ref[...])
    vf_out[...] = v
    vb_out[...] = v.astype(BF16)
    k = _dot(h, wk_ref[...])
    k_aug = ck_ref[...]
    for j, piece in enumerate(cum_pieces):
        k_aug = k_aug + _dot(piece, pk_ref[j])
    for n in range(kv_heads):
        sl = slice(n * LANES, (n + 1) * LANES)
        kn = _rms(k[:, sl], gk_ref[...], head_dim)
        kf_out[:, sl] = kn
        ka_out[:, sl] = (kn + k_aug[:, sl]).astype(BF16)
    scale = head_dim ** -0.5
    for p in range(heads // 2):
        cols = slice(p * 2 * LANES, (p + 1) * 2 * LANES)
        q2 = _dot(h, wq_ref[:, cols])
        q_aug = cq_ref[:, cols]
        for j, piece in enumerate(cum_pieces):
            q_aug = q_aug + _dot(piece, pq_ref[j, :, cols])
        for e in range(2):
            sl = slice(e * LANES, (e + 1) * LANES)
            out = slice((2 * p + e) * LANES, (2 * p + e + 1) * LANES)
            qa_out[:, out] = (_rms(q2[:, sl], gq_ref[...], head_dim) * scale + q_aug[:, sl]).astype(BF16)


def _fox_pre(x, gmix, w, seq_rows, heads, kv_heads, head_dim):
    r, d = x.shape
    tm = _tile(math.gcd(seq_rows, r), (512, 256, 128, 64, 32, 16, 8))
    qw, kw = heads * LANES, kv_heads * LANES
    row = lambda c: pl.BlockSpec((tm, c), lambda i: (i, 0))
    tri = (lax.broadcasted_iota(jnp.int32, (tm, tm), 0) >= lax.broadcasted_iota(jnp.int32, (tm, tm), 1)).astype(BF16)
    ins = [x, gmix, w["w_q"], w["w_k"], w["w_v"], w["w_f"], w["b_f"], w["g_q"], w["g_k"], tri,
           w["place_q"], w["place_k"], w["const_q"], w["const_k"]]
    return pl.pallas_call(
        functools.partial(_fox_pre_body, heads=heads, kv_heads=kv_heads, head_dim=head_dim,
                          tiles_per_seq=seq_rows // tm),
        grid=(r // tm,),
        in_specs=[row(d)] + [_full(a.shape) for a in ins[1:]],
        out_specs=[row(qw), row(kw), row(kw), row(kw), row(kw), row(LANES)],
        out_shape=[jax.ShapeDtypeStruct((r, qw), BF16), jax.ShapeDtypeStruct((r, kw), BF16),
                   jax.ShapeDtypeStruct((r, kw), BF16), jax.ShapeDtypeStruct((r, kw), F32),
                   jax.ShapeDtypeStruct((r, kw), F32), jax.ShapeDtypeStruct((r, LANES), F32)],
        scratch_shapes=[pltpu.VMEM((1, LANES), F32)],
        compiler_params=_params(1),
        name="fox_pre",
    )(*ins)


def _fox_dec_body(pt_ref, q_ref, lfr_ref, kn_ref, vn_ref, wsuf_ref, tblk_ref, ones_blk_ref, *rest, heads, t_new):
    pps = PAGES_PER_STEP
    k_refs, v_refs, lf_refs = rest[:pps], rest[pps:2 * pps], rest[2 * pps:3 * pps]
    o_ref, m_sc, l_sc, acc_sc, carry_sc, g_sc, s_sc, fn_sc = rest[3 * pps:]
    c = pl.program_id(1)
    rows = heads * t_new

    @pl.when(c == 0)
    def _():
        m_sc[...] = jnp.full_like(m_sc, NEG_INF)
        l_sc[...] = jnp.zeros_like(l_sc)
        acc_sc[...] = jnp.zeros_like(acc_sc)
        carry_sc[...] = jnp.zeros_like(carry_sc)
        fn = jnp.zeros((rows, LANES), F32)
        for piece in _split(lfr_ref[...]):
            fn = fn + _dot(tblk_ref[...], piece)
        fn_sc[...] = fn

    q = q_ref[...]
    fn = fn_sc[...]

    lf = jnp.concatenate([r[...] for r in lf_refs], axis=0)
    suf = jnp.zeros((pps * heads, 2 * PAGE_SIZE), F32)
    for piece in _split(lf):
        suf = suf + _dot(piece, wsuf_ref[...])
    running = carry_sc[...]
    for i in reversed(range(pps)):
        rsl = slice(i * heads, (i + 1) * heads)
        g_sc[:, i * PAGE_SIZE:(i + 1) * PAGE_SIZE] = suf[rsl, :PAGE_SIZE] + running
        running = running + suf[rsl, PAGE_SIZE:]
    carry_sc[...] = running
    k = jnp.concatenate([r[...] for r in k_refs], axis=0).astype(BF16)
    v = jnp.concatenate([r[...] for r in v_refs], axis=0).astype(BF16)
    s = _dot_nt(q, k)
    for h in range(heads):
        sl = slice(h * t_new, (h + 1) * t_new)
        s_sc[sl, :] = s[sl] + g_sc[h:h + 1, :] + fn[sl, :1]
    _softmax_update(s_sc[...], v, m_sc, l_sc, acc_sc)

    @pl.when(c == pl.num_programs(1) - 1)
    def _():
        rid = lax.broadcasted_iota(jnp.int32, (rows, LANES), 0) % t_new
        cid = lax.broadcasted_iota(jnp.int32, (rows, LANES), 1)
        diag = jnp.where(rid == cid, fn, 0.0)
        fcol = jnp.zeros((rows, LANES), F32)
        for piece in _split(diag):
            fcol = fcol + _dot(ones_blk_ref[...], piece)
        sn = _dot_nt(q, kn_ref[...]) + (fn[:, :1] - fcol)
        sn = jnp.where(cid <= rid, sn, NEG_INF)
        _softmax_update(sn, vn_ref[...], m_sc, l_sc, acc_sc)
        o_ref[...] = acc_sc[...] / l_sc[...]


def _fox_dec(page_table, q_bd, lf_rows, k_new, v_new, cache_k, cache_v, cache_lft, layer, heads, t_new):
    n, n_pages = page_table.shape
    pps = PAGES_PER_STEP
    assert n_pages % pps == 0
    nc = n_pages // pps
    rows = heads * t_new
    width = cache_k.shape[-1]
    chunk_keys = pps * PAGE_SIZE
    assert rows == LANES

    jj = lax.broadcasted_iota(jnp.int32, (PAGE_SIZE, 2 * PAGE_SIZE), 0)
    ss = lax.broadcasted_iota(jnp.int32, (PAGE_SIZE, 2 * PAGE_SIZE), 1)
    wsuf = ((jj > ss) | (ss >= PAGE_SIZE)).astype(BF16)
    ra = lax.broadcasted_iota(jnp.int32, (rows, rows), 0)
    rb = lax.broadcasted_iota(jnp.int32, (rows, rows), 1)
    same_head = (ra // t_new) == (rb // t_new)
    tblk = (same_head & (rb % t_new <= ra % t_new)).astype(BF16)
    ones_blk = same_head.astype(BF16)

    def page_spec(arr, i):
        return pl.BlockSpec((None, None) + arr.shape[2:],
                            lambda s, c, pt: (layer, pt[s * n_pages + (nc - 1 - c) * pps + i], 0, 0))

    const = lambda shape: pl.BlockSpec(shape, lambda s, c, pt: (0,) * len(shape))
    per_seq = lambda a: pl.BlockSpec((None,) + a.shape[1:], lambda s, c, pt: (s, 0, 0))
    in_specs = [per_seq(q_bd), per_seq(lf_rows), per_seq(k_new), per_seq(v_new),
                const(wsuf.shape), const(tblk.shape), const(ones_blk.shape)]
    in_specs += [page_spec(cache_k, i) for i in range(pps)] + [page_spec(cache_v, i) for i in range(pps)]
    in_specs += [page_spec(cache_lft, i) for i in range(pps)]
    grid_spec = pltpu.PrefetchScalarGridSpec(
        num_scalar_prefetch=1, grid=(n, nc), in_specs=in_specs,
        out_specs=pl.BlockSpec((None, rows, width), lambda s, c, pt: (s, 0, 0)),
        scratch_shapes=[pltpu.VMEM((rows, 1), F32), pltpu.VMEM((rows, 1), F32), pltpu.VMEM((rows, width), F32),
                        pltpu.VMEM((heads, PAGE_SIZE), F32), pltpu.VMEM((heads, chunk_keys), F32),
                        pltpu.VMEM((rows, chunk_keys), F32), pltpu.VMEM((rows, LANES), F32)])
    return pl.pallas_call(
        functools.partial(_fox_dec_body, heads=heads, t_new=t_new),
        grid_spec=grid_spec,
        out_shape=jax.ShapeDtypeStruct((n, rows, width), F32),
        compiler_params=_params(2),
        name="fox_dec",
    )(page_table.reshape(-1), q_bd, lf_rows, k_new, v_new, wsuf, tblk, ones_blk,
      *([cache_k] * pps), *([cache_v] * pps), *([cache_lft] * pps))


def _conv_glu_body(x_ref, gmix_ref, w_ref, b_ref, u_out):
    h = _rms(x_ref[...], gmix_ref[...]).astype(BF16)
    ch = u_out.shape[1]
    step = 2 * LANES
    for c in range(ch // step):
        a = _dot(h, w_ref[:, c * step:(c + 1) * step]) + b_ref[:, c * step:(c + 1) * step]
        gate = _dot(h, w_ref[:, ch + c * step:ch + (c + 1) * step]) + b_ref[:, ch + c * step:ch + (c + 1) * step]
        u_out[:, c * step:(c + 1) * step] = a * jax.nn.sigmoid(gate)


def _conv_glu(x, gmix, w, b):
    r, d = x.shape
    ch = w.shape[1] // 2
    tm = _tile(r, (512, 256, 128, 64, 32, 16, 8))
    return pl.pallas_call(
        _conv_glu_body,
        grid=(r // tm,),
        in_specs=[pl.BlockSpec((tm, d), lambda i: (i, 0)), _full((1, d)), _full(w.shape), _full((1, 2 * ch))],
        out_specs=pl.BlockSpec((tm, ch), lambda i: (i, 0)),
        out_shape=jax.ShapeDtypeStruct((r, ch), F32),
        compiler_params=_params(1),
        name="conv_glu",
    )(x, gmix, w.astype(BF16), b.reshape(1, -1))


def _ln_silu(z, g, b):
    mu = jnp.mean(z, axis=-1, keepdims=True)
    zc = z - mu
    y = zc * lax.rsqrt(jnp.mean(zc * zc, axis=-1, keepdims=True) + NORM_EPS) * g + b
    return y * jax.nn.sigmoid(y)


HALO = 32


def _conv_dw_prompt_body(u_ref, halo_ref, w_ref, bdw_ref, g_ref, b_ref, z_out, ext_sc, *, width, tiles_per_seq):
    i = pl.program_id(0)
    tm = u_ref.shape[0]
    first = i % tiles_per_seq == 0
    ext_sc[:HALO, :] = jnp.where(first, 0.0, halo_ref[...])
    ext_sc[HALO:, :] = u_ref[...]
    acc = jnp.zeros(u_ref.shape, F32) + bdw_ref[...]
    off = HALO - (width - 1)
    for w in range(width):
        acc = acc + ext_sc[off + w:off + w + tm, :] * w_ref[w:w + 1, :]
    z_out[...] = _ln_silu(acc, g_ref[...], b_ref[...]).astype(BF16)


def _conv_dw_prompt(u, w_dw, b_dw, g_ln, b_ln, seq_rows):
    r, ch = u.shape
    width = w_dw.shape[0]
    assert width - 1 <= HALO
    tm = _tile(seq_rows, (256, 128, 64, 32))
    ratio = tm // HALO
    vec = lambda a: a.reshape(1, ch)
    return pl.pallas_call(
        functools.partial(_conv_dw_prompt_body, width=width, tiles_per_seq=seq_rows // tm),
        grid=(r // tm,),
        in_specs=[pl.BlockSpec((tm, ch), lambda i: (i, 0)),
                  pl.BlockSpec((HALO, ch), lambda i: (jnp.maximum(i * ratio - 1, 0), 0)),
                  _full((width, ch)), _full((1, ch)), _full((1, ch)), _full((1, ch))],
        out_specs=pl.BlockSpec((tm, ch), lambda i: (i, 0)),
        out_shape=jax.ShapeDtypeStruct((r, ch), BF16),
        scratch_shapes=[pltpu.VMEM((tm + HALO, ch), F32)],
        compiler_params=_params(1),
        name="conv_dw_prompt",
    )(u, u, w_dw, vec(b_dw), vec(g_ln), vec(b_ln))


def _conv_dw_sample_body(ext_ref, w_ref, bdw_ref, g_ref, b_ref, z_out, *, width, t_new):
    acc = jnp.zeros(z_out.shape, F32) + bdw_ref[...]
    for w in range(width):
        acc = acc + ext_ref[:, w:w + t_new, :] * w_ref[w:w + 1, :]
    z_out[...] = _ln_silu(acc, g_ref[...], b_ref[...]).astype(BF16)


def _conv_dw_sample(ext, w_dw, b_dw, g_ln, b_ln, t_new):
    n, rows, ch = ext.shape
    width = w_dw.shape[0]
    nb = _tile(n, (16, 8, 4, 2, 1))
    vec = lambda a: a.reshape(1, ch)
    return pl.pallas_call(
        functools.partial(_conv_dw_sample_body, width=width, t_new=t_new),
        grid=(n // nb,),
        in_specs=[pl.BlockSpec((nb, rows, ch), lambda i: (i, 0, 0)),
                  _full((width, ch)), _full((1, ch)), _full((1, ch)), _full((1, ch))],
        out_specs=pl.BlockSpec((nb, t_new, ch), lambda i: (i, 0, 0)),
        out_shape=jax.ShapeDtypeStruct((n, t_new, ch), BF16),
        compiler_params=_params(1),
        name="conv_dw_sample",
    )(ext, w_dw, vec(b_dw), vec(g_ln), vec(b_ln))


def _pad_last(a, n):
    return jnp.pad(a, [(0, 0)] * (a.ndim - 1) + [(0, n - a.shape[-1])])


def _rope_tables(pos, half):
    inv = ROPE_BASE ** (-jnp.arange(half, dtype=F32) / half)
    ang = pos.astype(F32)[:, None] * inv[None, :]
    return jnp.cos(ang), jnp.sin(ang)


def _mla_layer(x, gmix, pos_rows, dims, cache_lat, cache_kpe, layer, page_table,
               w_dq, g_qa, w_uq, g_qn, w_dkv, g_kva, w_ukv, g_kn, w_o):
    batch, seq, n, t_new = dims
    d = x.shape[1]
    heads, qk_dim = w_uq.shape[1], w_uq.shape[2]
    kv_lora = w_ukv.shape[0]
    nope = w_ukv.shape[2] - w_o.shape[1]
    v_dim = w_o.shape[1]
    rope_dim = qk_dim - nope
    half = rope_dim // 2
    assert nope + rope_dim <= LANES and nope % 8 == 0 and nope == 64 and rope_dim == 32
    bs = batch * seq
    past = page_table.shape[1] * PAGE_SIZE

    w_dkv_pad = jnp.concatenate([w_dkv[:, :kv_lora], jnp.zeros((d, nope), F32), w_dkv[:, kv_lora:],
                                 jnp.zeros((d, LANES - qk_dim), F32)], axis=1)
    gslot = lambda g: _pad_last(g.reshape(1, qk_dim), LANES)
    w = dict(
        w_dq=w_dq.astype(BF16), g_qa=g_qa.reshape(1, -1),
        w_uq=_pad_last(w_uq, LANES).reshape(w_uq.shape[0], heads * LANES).astype(BF16), g_q=gslot(g_qn),
        w_dkv=w_dkv_pad.astype(BF16), g_kva=g_kva.reshape(1, -1),
        w_uk=_pad_last(w_ukv[:, :, :nope], LANES).reshape(kv_lora, heads * LANES).astype(BF16),
        w_uv=_pad_last(w_ukv[:, :, nope:], LANES).reshape(kv_lora, heads * LANES).astype(BF16), g_k=gslot(g_kn))
    cos, sin = _rope_tables(pos_rows, half)
    r = x.shape[0]
    cos_slot = jnp.concatenate([jnp.ones((r, nope), F32), cos, cos, jnp.ones((r, LANES - qk_dim), F32)], axis=1)
    sin_slot = jnp.concatenate([jnp.zeros((r, nope), F32), -sin, sin, jnp.zeros((r, LANES - qk_dim), F32)], axis=1)
    q, k, v, lat, kpe_slot = _mla_pre(x, gmix.reshape(1, d), w, cos_slot, sin_slot, heads, qk_dim)
    kpe = kpe_slot[:, nope:qk_dim]

    o_p = _flash(q[:bs], k[:bs], v[:bs], batch, seq, heads, 1)

    wukt = jnp.transpose(w_ukv[:, :, :nope], (1, 2, 0))
    wcat = jnp.zeros((heads, LANES, kv_lora + LANES), F32)
    wcat = wcat.at[:, :nope, :kv_lora].set(wukt)
    wcat = wcat.at[:, nope:qk_dim, kv_lora:kv_lora + rope_dim].set(jnp.eye(rope_dim, dtype=F32))
    qdec = _mla_qdec(q[bs:], gslot(g_kn), wcat.astype(BF16)).reshape(heads, n, t_new, kv_lora + LANES)
    cos_t, sin_t = (a.T for a in _rope_tables(jnp.arange(past), half))
    cos_n, sin_n = (a.T for a in _rope_tables(past + jnp.arange(PAGE_SIZE), half))
    pad_new = lambda a: jnp.pad(a.reshape(n, t_new, -1), ((0, 0), (0, PAGE_SIZE - t_new), (0, 0))).astype(BF16)
    o_lat = _mla_dec(page_table, qdec, wukt.reshape(heads * nope, kv_lora).astype(BF16),
                     g_kn[nope:].reshape(rope_dim, 1), jnp.eye(rope_dim, dtype=BF16), cos_t, sin_t, cos_n, sin_n,
                     pad_new(lat[bs:]), pad_new(kpe[bs:]), cache_lat, cache_kpe, layer, heads, qk_dim, nope)
    w_uv_dec = _pad_last(jnp.transpose(w_ukv[:, :, nope:], (1, 0, 2)), LANES).astype(BF16)
    o_s = _mla_unabsorb(o_lat.reshape(n, heads, t_new, kv_lora), w_uv_dec)

    w_o_pad = jnp.pad(w_o, ((0, 0), (0, LANES - v_dim), (0, 0))).reshape(heads * LANES, d)
    x = _proj_res(x, jnp.concatenate([o_p, o_s], axis=0), w_o_pad, jnp.zeros((d,), F32))
    return x, lat, kpe


def _fox_layer(x, gmix, dims, cache_k, cache_v, cache_lf, layer, page_table,
               w_q, w_k, w_v, w_f, b_f, g_qn, g_kn, w_o):
    batch, seq, n, t_new = dims
    d = x.shape[1]
    heads, head_dim = w_q.shape[1], w_q.shape[2]
    kv_heads = w_k.shape[1]
    group = heads // kv_heads
    bs = batch * seq
    sp = FORGET_SPLIT
    base = head_dim
    assert base + sp * (group + 1) <= LANES and heads <= LANES

    place_q = jnp.zeros((sp, LANES, heads * LANES), F32)
    place_k = jnp.zeros((sp, LANES, kv_heads * LANES), F32)
    const_q = jnp.zeros((1, heads * LANES), F32)
    const_k = jnp.zeros((1, kv_heads * LANES), F32)
    hh = jnp.arange(heads)
    for j in range(sp):
        place_q = place_q.at[j, hh, hh * LANES + base + sp * group + j].set(1.0)
        place_k = place_k.at[j, hh, (hh // group) * LANES + base + sp * (hh % group) + j].set(-1.0)
        const_q = const_q.at[0, hh * LANES + base + sp * (hh % group) + j].set(1.0)
        const_k = const_k.at[0, jnp.arange(kv_heads) * LANES + base + sp * group + j].set(1.0)
    gslot = lambda g: _pad_last(g.reshape(1, head_dim), LANES)
    w = dict(w_q=_pad_last(w_q, LANES).reshape(d, heads * LANES).astype(BF16),
             w_k=_pad_last(w_k, LANES).reshape(d, kv_heads * LANES).astype(BF16),
             w_v=_pad_last(w_v, LANES).reshape(d, kv_heads * LANES).astype(BF16),
             w_f=_pad_last(w_f, LANES).astype(BF16), b_f=_pad_last(b_f.reshape(1, heads), LANES),
             g_q=gslot(g_qn), g_k=gslot(g_kn), place_q=place_q.astype(BF16), place_k=place_k.astype(BF16),
             const_q=const_q, const_k=const_k)
    q_aug, k_aug, v_b, k_f, v_f, lf = _fox_pre(x, gmix.reshape(1, d), w, seq, heads, kv_heads, head_dim)
    unslot = lambda a, nh: a.reshape(a.shape[0], nh, LANES)[:, :, :head_dim]
    k_out, v_out, lf_out = unslot(k_f, kv_heads), unslot(v_f, kv_heads), lf[:, :heads]

    o_p = _flash(q_aug[:bs], k_aug[:bs], v_b[:bs], batch, seq, heads, group)

    kvw = kv_heads * head_dim
    q_s = unslot(q_aug[bs:], heads).reshape(n, t_new, kv_heads, group, head_dim)
    q_bd = jnp.einsum("ntkgd,kj->nkgtjd", q_s.astype(F32), jnp.eye(kv_heads, dtype=F32))
    q_bd = q_bd.reshape(n, heads * t_new, kvw).astype(BF16)
    lf_rows = jnp.broadcast_to(jnp.transpose(lf_out[bs:].reshape(n, t_new, heads), (0, 2, 1))[..., None],
                               (n, heads, t_new, LANES)).reshape(n, heads * t_new, LANES)
    pad_new = lambda a: jnp.pad(a.reshape(n, t_new, kvw), ((0, 0), (0, PAGE_SIZE - t_new), (0, 0))).astype(BF16)
    pool = cache_k.shape[1]
    o_bd = _fox_dec(page_table, q_bd, lf_rows, pad_new(k_out[bs:]), pad_new(v_out[bs:]),
                    cache_k.reshape(cache_k.shape[0], pool, PAGE_SIZE, kvw),
                    cache_v.reshape(cache_v.shape[0], pool, PAGE_SIZE, kvw),
                    jnp.transpose(cache_lf, (0, 1, 3, 2)), layer, heads, t_new)
    o_s = jnp.einsum("nkgtjd,kj->ntkgd", o_bd.reshape(n, kv_heads, group, t_new, kv_heads, head_dim),
                     jnp.eye(kv_heads, dtype=F32))
    o_s = _pad_last(o_s.reshape(n * t_new, heads, head_dim), LANES).reshape(n * t_new, heads * LANES).astype(BF16)

    w_o_pad = jnp.pad(w_o, ((0, 0), (0, LANES - head_dim), (0, 0))).reshape(heads * LANES, d)
    x = _proj_res(x, jnp.concatenate([o_p, o_s], axis=0), w_o_pad, jnp.zeros((d,), F32))
    return x, k_out, v_out, lf_out


def _conv_layer(x, gmix, dims, state, w_pw1, b_pw1, w_dw, b_dw, g_ln, b_ln, w_pw2, b_pw2):
    batch, seq, n, t_new = dims
    d = x.shape[1]
    bs = batch * seq
    keep = w_dw.shape[0] - 1
    assert seq >= keep
    u = _conv_glu(x, gmix.reshape(1, d), w_pw1, b_pw1)
    ch = u.shape[1]
    z_p = _conv_dw_prompt(u[:bs], w_dw, b_dw, g_ln, b_ln, seq)
    ext_s = jnp.concatenate([state.astype(F32), u[bs:].reshape(n, t_new, ch)], axis=1)
    rows = -(-(keep + t_new) // 8) * 8
    z_s = _conv_dw_sample(jnp.pad(ext_s, ((0, 0), (0, rows - keep - t_new), (0, 0))), w_dw, b_dw, g_ln, b_ln, t_new)
    z = jnp.concatenate([z_p, z_s.reshape(n * t_new, ch)], axis=0)
    x = _proj_res(x, z, w_pw2, b_pw2)
    return x, u[:bs].reshape(batch, seq, ch)[:, seq - keep:], ext_s[:, t_new:]


def kernel(x_prompt, x_sample, cache_mla_latent, cache_mla_krope, cache_fox_k, cache_fox_v, cache_fox_logf, state_conv, page_table, norm_mix, norm_mlp, mlp_up, mlp_down, mla_w_dq, mla_g_qa, mla_w_uq, mla_g_qn, mla_w_dkv, mla_g_kva, mla_w_ukv, mla_g_kn, mla_w_o, fox_w_q, fox_w_k, fox_w_v, fox_w_f, fox_b_f, fox_g_qn, fox_g_kn, fox_w_o, conv_w_pw1, conv_b_pw1, conv_w_dw, conv_b_dw, conv_g_ln, conv_b_ln, conv_w_pw2, conv_b_pw2):
    batch, seq, d = x_prompt.shape
    n, t_new, _ = x_sample.shape
    dims = (batch, seq, n, t_new)
    bs = batch * seq
    past = page_table.shape[1] * PAGE_SIZE
    depth = norm_mix.shape[0]
    x = jnp.concatenate([x_prompt.reshape(bs, d), x_sample.reshape(n * t_new, d)], axis=0)
    pos_rows = jnp.concatenate([jnp.tile(jnp.arange(seq), batch), jnp.tile(past + jnp.arange(t_new), n)])

    outs = {name: [] for name in ("lat", "kpe", "fk", "fv", "flf", "cp", "cs")}
    for i in range(depth):
        kind, j = i % 3, i // 3
        if kind == 0:
            x, lat, kpe = _mla_layer(x, norm_mix[i], pos_rows, dims, cache_mla_latent, cache_mla_krope, j, page_table,
                                     mla_w_dq[j], mla_g_qa[j], mla_w_uq[j], mla_g_qn[j], mla_w_dkv[j], mla_g_kva[j],
                                     mla_w_ukv[j], mla_g_kn[j], mla_w_o[j])
            outs["lat"].append(lat)
            outs["kpe"].append(kpe)
        elif kind == 1:
            x, fk, fv, flf = _fox_layer(x, norm_mix[i], dims, cache_fox_k, cache_fox_v, cache_fox_logf, j, page_table,
                                        fox_w_q[j], fox_w_k[j], fox_w_v[j], fox_w_f[j], fox_b_f[j], fox_g_qn[j],
                                        fox_g_kn[j], fox_w_o[j])
            outs["fk"].append(fk)
            outs["fv"].append(fv)
            outs["flf"].append(flf)
        else:
            x, cp, cs = _conv_layer(x, norm_mix[i], dims, state_conv[j], conv_w_pw1[j], conv_b_pw1[j], conv_w_dw[j],
                                    conv_b_dw[j], conv_g_ln[j], conv_b_ln[j], conv_w_pw2[j], conv_b_pw2[j])
            outs["cp"].append(cp)
            outs["cs"].append(cs)
        x = _mlp(x, norm_mlp[i], mlp_up[i], mlp_down[i])

    def split_rows(name):
        stacked = jnp.stack(outs[name])
        tail = stacked.shape[2:]
        return (stacked[:, :bs].reshape((-1, batch, seq) + tail), stacked[:, bs:].reshape((-1, n, t_new) + tail))

    lat_p, lat_s = split_rows("lat")
    kpe_p, kpe_s = split_rows("kpe")
    fk_p, fk_s = split_rows("fk")
    fv_p, fv_s = split_rows("fv")
    flf_p, flf_s = split_rows("flf")
    return (x[:bs].reshape(batch, seq, d), x[bs:].reshape(n, t_new, d), lat_p, lat_s, kpe_p, kpe_s,
            fk_p, fk_s, fv_p, fv_s, flf_p, flf_s, jnp.stack(outs["cp"]), jnp.stack(outs["cs"]))
```

```python
import functools
import math

import jax
import jax.numpy as jnp
from jax import lax
from jax.experimental import pallas as pl
from jax.experimental.pallas import tpu as pltpu

F32 = jnp.float32
BF16 = jnp.bfloat16

NORM_EPS = 1e-6
NEG_INF = -1e30
ROPE_BASE = 10000.0
LOG2E = math.log2(math.e)
FORGET_SPLIT = 3
LANES = 128
PAGE_SIZE = 128
PAGES_PER_STEP = 8
DEC_STREAMS = 2
VMEM_LIMIT = 56 * 1024 * 1024

NT_DIMS = (((1,), (1,)), ((), ()))


def _params(n_axes):
    return pltpu.CompilerParams(dimension_semantics=("arbitrary",) * n_axes, vmem_limit_bytes=VMEM_LIMIT)


def _tile(n, cands):
    for c in cands:
        if n % c == 0:
            return c
    raise ValueError(f"no tile for {n}")


def _dot(a, b):
    return jnp.dot(a, b, preferred_element_type=F32)


def _dot_nt(a, b):
    return lax.dot_general(a, b, NT_DIMS, preferred_element_type=F32)


def _rms(x, g, n=None):
    n = x.shape[-1] if n is None else n
    ss = jnp.sum(x * x, axis=-1, keepdims=True)
    return x * lax.rsqrt(ss * (1.0 / n) + NORM_EPS) * g


def _split(x, pieces=FORGET_SPLIT):
    out = []
    for _ in range(pieces):
        p = x.astype(BF16)
        out.append(p)
        x = x - p.astype(F32)
    return out


def _full(shape):
    nd = len(shape)
    return pl.BlockSpec(shape, lambda *_: (0,) * nd)


def _mlp_body(x_ref, a_ref, wa_ref, ba_ref, g_ref, wu_ref, wd_ref, o_ref, x1_sc, h_sc, acc_sc):
    f = pl.program_id(1)

    @pl.when(f == 0)
    def _():
        x1 = x_ref[...] + _dot(a_ref[...], wa_ref[...]) + ba_ref[...]
        x1_sc[...] = x1
        h_sc[...] = _rms(x1, g_ref[...]).astype(BF16)
        acc_sc[...] = jnp.zeros_like(acc_sc)

    a = jnp.maximum(_dot(h_sc[...], wu_ref[...]), 0.0)
    acc_sc[...] += _dot((a * a).astype(BF16), wd_ref[...])

    @pl.when(f == pl.num_programs(1) - 1)
    def _():
        o_ref[...] = x1_sc[...] + acc_sc[...]


def _mlp(x, a, w_a, b_a, g, w_up, w_down):
    r, d = x.shape
    k = a.shape[1]
    dff = w_up.shape[1]
    tm = _tile(r, (512, 256, 128, 64, 32, 16, 8))
    tf = _tile(dff, (1024, 512, 256, 128))
    return pl.pallas_call(
        _mlp_body,
        grid=(r // tm, dff // tf),
        in_specs=[pl.BlockSpec((tm, d), lambda i, f: (i, 0)),
                  pl.BlockSpec((tm, k), lambda i, f: (i, 0)),
                  pl.BlockSpec((k, d), lambda i, f: (0, 0)),
                  pl.BlockSpec((1, d), lambda i, f: (0, 0)),
                  pl.BlockSpec((1, d), lambda i, f: (0, 0)),
                  pl.BlockSpec((d, tf), lambda i, f: (0, f)),
                  pl.BlockSpec((tf, d), lambda i, f: (f, 0))],
        out_specs=pl.BlockSpec((tm, d), lambda i, f: (i, 0)),
        out_shape=jax.ShapeDtypeStruct((r, d), F32),
        scratch_shapes=[pltpu.VMEM((tm, d), F32), pltpu.VMEM((tm, d), BF16), pltpu.VMEM((tm, d), F32)],
        compiler_params=_params(2),
        name="mlp",
    )(x, a, w_a.astype(BF16), b_a.reshape(1, d).astype(F32), g.reshape(1, d), w_up.astype(BF16), w_down.astype(BF16))


def _flash_body(q_ref, k_ref, v_ref, o_ref, m_sc, acc_sc, *, t, kv_of, sum_lane):
    qi = pl.program_id(2)
    streams = len(kv_of)
    m_sc[...] = jnp.full_like(m_sc, NEG_INF)
    acc_sc[...] = jnp.zeros_like(acc_sc)

    def step(j, masked):
        start = pl.multiple_of(j * t, t)
        for g in range(streams):
            kv = slice(kv_of[g] * LANES, (kv_of[g] + 1) * LANES)
            s = _dot_nt(q_ref[:, g * LANES:(g + 1) * LANES], k_ref[pl.ds(start, t), kv])
            if masked:
                row = lax.broadcasted_iota(jnp.int32, (t, t), 0)
                col = lax.broadcasted_iota(jnp.int32, (t, t), 1)
                s = jnp.where(col <= row, s, NEG_INF)
            m_prev = m_sc[g]
            m_new = jnp.maximum(m_prev, jnp.max(s, axis=-1, keepdims=True))
            p = jnp.exp2(s - m_new).astype(BF16)
            acc_sc[g] = jnp.exp2(m_prev - m_new) * acc_sc[g] + _dot(p, v_ref[pl.ds(start, t), kv])
            m_sc[g] = m_new

    def unmasked(j, carry):
        step(j, False)
        return carry

    lax.fori_loop(0, qi, unmasked, 0)
    step(qi, True)
    for g in range(streams):
        acc = acc_sc[g]
        o_ref[:, g * LANES:(g + 1) * LANES] = (acc / acc[:, sum_lane:sum_lane + 1]).astype(o_ref.dtype)


def _flash_shifted_body(q_ref, k_ref, v_ref, o_ref, acc_sc, *, t, kv_of, sum_lane):
    qi = pl.program_id(2)
    streams = len(kv_of)
    acc_sc[...] = jnp.zeros_like(acc_sc)

    def step(j, masked):
        start = pl.multiple_of(j * t, t)
        for g in range(streams):
            kv = slice(kv_of[g] * LANES, (kv_of[g] + 1) * LANES)
            s = _dot_nt(q_ref[:, g * LANES:(g + 1) * LANES], k_ref[pl.ds(start, t), kv])
            if masked:
                row = lax.broadcasted_iota(jnp.int32, (t, t), 0)
                col = lax.broadcasted_iota(jnp.int32, (t, t), 1)
                s = jnp.where(col <= row, s, NEG_INF)
            acc_sc[g] += _dot(jnp.exp2(s).astype(BF16), v_ref[pl.ds(start, t), kv])

    def unmasked(j, carry):
        step(j, False)
        return carry

    lax.fori_loop(0, qi, unmasked, 0)
    step(qi, True)
    for g in range(streams):
        acc = acc_sc[g]
        o_ref[:, g * LANES:(g + 1) * LANES] = (acc / acc[:, sum_lane:sum_lane + 1]).astype(o_ref.dtype)


MAX_SHIFT = 60.0


def _score_shift(g_q, g_k, n, scale):
    bound = n * jnp.max(jnp.abs(g_q)) * jnp.max(jnp.abs(g_k)) * scale
    return jnp.ceil(bound * 1.02 + 0.5)


def _flash(q, k, v, shift, batch, seq, heads, group, streams, sum_lane):
    t = _tile(seq, (512, 256, 128, 64, 32, 16))
    nq = seq // t
    assert heads % streams == 0 and (streams % group == 0 or group % streams == 0)
    if streams >= group:
        kv_w, kv_of = streams // group, tuple(g // group for g in range(streams))
        kv_idx = lambda hb: hb
    else:
        kv_w, kv_of = 1, (0,) * streams
        kv_idx = lambda hb: hb * streams // group

    def call(body, scratch, name):
        return pl.pallas_call(
            functools.partial(body, t=t, kv_of=kv_of, sum_lane=sum_lane),
            grid=(batch, heads // streams, nq),
            in_specs=[pl.BlockSpec((t, streams * LANES), lambda b, h, i: (b * nq + i, h)),
                      pl.BlockSpec((seq, kv_w * LANES), lambda b, h, i: (b, kv_idx(h))),
                      pl.BlockSpec((seq, kv_w * LANES), lambda b, h, i: (b, kv_idx(h)))],
            out_specs=pl.BlockSpec((t, streams * LANES), lambda b, h, i: (b * nq + i, h)),
            out_shape=jax.ShapeDtypeStruct((batch * seq, heads * LANES), BF16),
            scratch_shapes=scratch,
            compiler_params=_params(3),
            name=name,
        )

    acc = pltpu.VMEM((streams, t, LANES), F32)
    shifted = call(_flash_shifted_body, [acc], "flash_shifted")
    online = call(_flash_body, [pltpu.VMEM((streams, t, 1), F32), acc], "flash")
    return lax.cond(shift <= MAX_SHIFT, shifted, online, q, k, v)


def _sum_lane_row(heads, lane):
    return jnp.zeros((heads, LANES), F32).at[:, lane].set(1.0).reshape(1, heads * LANES)


def _rope_slot(x, cos, sin, lane):
    swapped = jnp.where(lane < 80, pltpu.roll(x, 112, 1), pltpu.roll(x, 16, 1))
    return x * cos + swapped * sin


def _mla_pre_body(x_ref, gmix_ref, wdq_ref, gqa_ref, wuq_ref, gq_ref, wdkv_ref, gkva_ref, wuk_ref, wuv_ref,
                  gk_ref, vone_ref, qone_ref, kshift_ref, cos_ref, sin_ref, q_out, k_out, v_out, lat_out, kpe_out,
                  *, heads, qk_dim):
    h = _rms(x_ref[...], gmix_ref[...]).astype(BF16)
    cq = _rms(_dot(h, wdq_ref[...]), gqa_ref[...]).astype(BF16)
    a = _dot(h, wdkv_ref[...])
    kv_lora = gkva_ref.shape[1]
    lat = _rms(a[:, :kv_lora], gkva_ref[...])
    lat_out[...] = lat
    kpe = a[:, kv_lora:]
    kpe_out[...] = kpe
    latb = lat.astype(BF16)
    cos, sin = cos_ref[...], sin_ref[...]
    lane = lax.broadcasted_iota(jnp.int32, (1, LANES), 1)
    scale = qk_dim ** -0.5 * LOG2E
    for p in range(heads // 2):
        cols = slice(p * 2 * LANES, (p + 1) * 2 * LANES)
        q2 = _dot(cq, wuq_ref[:, cols])
        k2 = _dot(latb, wuk_ref[:, cols])
        v_out[:, cols] = (_dot(latb, wuv_ref[:, cols]) + vone_ref[:, cols]).astype(BF16)
        for e in range(2):
            sl = slice(e * LANES, (e + 1) * LANES)
            out = slice((2 * p + e) * LANES, (2 * p + e + 1) * LANES)
            qh = _rms(q2[:, sl], gq_ref[...], qk_dim) * scale
            q_out[:, out] = (_rope_slot(qh, cos, sin, lane) + qone_ref[...]).astype(BF16)
            kh = _rms(k2[:, sl] + kpe, gk_ref[...], qk_dim)
            k_out[:, out] = (_rope_slot(kh, cos, sin, lane) + kshift_ref[...]).astype(BF16)


def _mla_pre(x, gmix, w, cos, sin, heads, qk_dim):
    r, d = x.shape
    tm = _tile(r, (512, 256, 128, 64, 32, 16, 8))
    hw = heads * LANES
    row = lambda c: pl.BlockSpec((tm, c), lambda i: (i, 0))
    ins = [x, gmix, w["w_dq"], w["g_qa"], w["w_uq"], w["g_q"], w["w_dkv"], w["g_kva"], w["w_uk"], w["w_uv"], w["g_k"],
           w["v_one"], w["q_one"], w["k_shift"]]
    return pl.pallas_call(
        functools.partial(_mla_pre_body, heads=heads, qk_dim=qk_dim),
        grid=(r // tm,),
        in_specs=[row(d)] + [_full(a.shape) for a in ins[1:]] + [row(LANES), row(LANES)],
        out_specs=[row(hw), row(hw), row(hw), row(w["g_kva"].shape[1]), row(LANES)],
        out_shape=[jax.ShapeDtypeStruct((r, hw), BF16)] * 3
        + [jax.ShapeDtypeStruct((r, w["g_kva"].shape[1]), F32), jax.ShapeDtypeStruct((r, LANES), F32)],
        compiler_params=_params(1),
        name="mla_pre",
    )(*ins, cos, sin)


def _mla_qdec_body(q_ref, gk_ref, w_ref, o_ref):
    o_ref[...] = _dot((q_ref[...].astype(F32) * gk_ref[...]).astype(BF16), w_ref[...])


def _mla_qdec(q_s, gk, wcat):
    nt = q_s.shape[0]
    heads, _, wd = wcat.shape
    return pl.pallas_call(
        _mla_qdec_body,
        grid=(heads,),
        in_specs=[pl.BlockSpec((nt, LANES), lambda h: (0, h)), _full((1, LANES)),
                  pl.BlockSpec((None, LANES, wd), lambda h: (h, 0, 0))],
        out_specs=pl.BlockSpec((None, nt, wd), lambda h: (h, 0, 0)),
        out_shape=jax.ShapeDtypeStruct((heads, nt, wd), F32),
        compiler_params=_params(1),
        name="mla_qdec",
    )(q_s, gk, wcat)


def _mla_unabsorb_body(o_ref, w_ref, out_ref):
    n, t, c = o_ref.shape
    out_ref[...] = _dot(o_ref[...].reshape(n * t, c).astype(BF16), w_ref[...]).astype(BF16)


def _mla_unabsorb(o_lat, w_uv):
    n, heads, t, c = o_lat.shape
    return pl.pallas_call(
        _mla_unabsorb_body,
        grid=(heads,),
        in_specs=[pl.BlockSpec((n, None, t, c), lambda h: (0, h, 0, 0)),
                  pl.BlockSpec((None, c, LANES), lambda h: (h, 0, 0))],
        out_specs=pl.BlockSpec((n * t, LANES), lambda h: (0, h)),
        out_shape=jax.ShapeDtypeStruct((n * t, heads * LANES), BF16),
        compiler_params=_params(1),
        name="mla_unabsorb",
    )(o_lat, w_uv)


def _softmax_update(s, v, m_sc, l_sc, acc_sc, st, v_transposed=False):
    m_prev = m_sc[st]
    m_new = jnp.maximum(m_prev, jnp.max(s, axis=-1, keepdims=True))
    alpha = jnp.exp2(m_prev - m_new)
    p = jnp.exp2(s - m_new)
    l_sc[st] = alpha * l_sc[st] + jnp.sum(p, axis=-1, keepdims=True)
    pv = _dot_nt(p.astype(BF16), v) if v_transposed else _dot(p.astype(BF16), v)
    acc_sc[st] = alpha * acc_sc[st] + pv
    m_sc[st] = m_new


def _softmax_init(m_sc, l_sc, acc_sc):
    m_sc[...] = jnp.full_like(m_sc, NEG_INF)
    l_sc[...] = jnp.zeros_like(l_sc)
    acc_sc[...] = jnp.zeros_like(acc_sc)


def _softmax_merge(m_sc, l_sc, acc_sc):
    m = m_sc[0]
    for st in range(1, DEC_STREAMS):
        m = jnp.maximum(m, m_sc[st])
    l = jnp.zeros_like(m)
    acc = jnp.zeros(acc_sc.shape[1:], F32)
    for st in range(DEC_STREAMS):
        w = jnp.exp2(m_sc[st] - m)
        l = l + w * l_sc[st]
        acc = acc + w * acc_sc[st]
    return acc / l


def _mla_dec_body(pt_ref, q_ref, wukt_ref, gkr_ref, cos_ref, sin_ref, cosn_ref, sinn_ref, latn_ref, kpen_ref,
                  *rest, heads, t_new, qk_dim, nope):
    pps = PAGES_PER_STEP
    lat_refs, kpe_refs = rest[:pps], rest[pps:2 * pps]
    o_ref, m_sc, l_sc, acc_sc, rinv_sc, s_sc = rest[2 * pps:]
    c = pl.program_id(1)
    kv_lora = wukt_ref.shape[1]
    rows = heads * t_new

    @pl.when(c == 0)
    def _():
        _softmax_init(m_sc, l_sc, acc_sc)

    q = q_ref[...].reshape(rows, q_ref.shape[-1]).astype(BF16)
    q_lat, q_rope = q[:, :kv_lora], q[:, kv_lora:kv_lora + LANES]
    rope_dim = qk_dim - nope
    half = rope_dim // 2

    def chunk(lat, kpt, cos, sin, st, new):
        n_keys = lat.shape[0]
        kt = _dot_nt(wukt_ref[...], lat)
        ss = jnp.sum((kt * kt).reshape(heads, nope, n_keys), axis=1)
        ss = ss + jnp.sum(kpt * kpt, axis=0, keepdims=True)
        rinv_sc[st, :, :n_keys] = lax.rsqrt(ss * (1.0 / qk_dim) + NORM_EPS)
        kg = kpt * gkr_ref[...]
        x1, x2 = kg[:half], kg[half:]
        krot = jnp.concatenate([x1 * cos - x2 * sin, x2 * cos + x1 * sin,
                                jnp.zeros((LANES - rope_dim, n_keys), F32)], axis=0).astype(BF16)
        s = _dot_nt(q_lat, lat) + _dot(q_rope, krot)
        for h in range(heads):
            sl = slice(h * t_new, (h + 1) * t_new)
            s_sc[st, sl, :n_keys] = s[sl] * rinv_sc[st, h:h + 1, :n_keys]
        s = s_sc[st, :, :n_keys]
        if new:
            tq = lax.broadcasted_iota(jnp.int32, (rows, n_keys), 0) % t_new
            tk = lax.broadcasted_iota(jnp.int32, (rows, n_keys), 1)
            s = jnp.where(tk <= tq, s, NEG_INF)
        _softmax_update(s, lat, m_sc, l_sc, acc_sc, st)

    per = pps // DEC_STREAMS
    for st in range(DEC_STREAMS):
        pages = range(st * per, (st + 1) * per)
        lat = jnp.concatenate([lat_refs[i][...] for i in pages], axis=0).astype(BF16)
        kpt = jnp.concatenate([kpe_refs[i][...] for i in pages], axis=1)
        lanes = slice(st * per * PAGE_SIZE, (st + 1) * per * PAGE_SIZE)
        chunk(lat, kpt, cos_ref[:, lanes], sin_ref[:, lanes], st, False)

    @pl.when(c == pl.num_programs(1) - 1)
    def _():
        chunk(latn_ref[...], kpen_ref[...], cosn_ref[...], sinn_ref[...], 0, True)
        o_ref[...] = _softmax_merge(m_sc, l_sc, acc_sc)


def _mla_dec(page_table, qdec, wukt, gkr, cos_t, sin_t, cos_n, sin_n, lat_new, kpe_new_t, cache_lat, cache_kpe_t,
             layer, heads, qk_dim, nope):
    n, n_pages = page_table.shape
    t_new = qdec.shape[2]
    kv_lora = wukt.shape[1]
    rope_dim = qk_dim - nope
    pps = PAGES_PER_STEP
    assert n_pages % pps == 0 and pps % DEC_STREAMS == 0 and t_new <= PAGE_SIZE
    nc = n_pages // pps
    rows = heads * t_new
    chunk_keys = pps * PAGE_SIZE
    stream_keys = chunk_keys // DEC_STREAMS

    def page_spec(shape, i):
        return pl.BlockSpec((None, None) + shape, lambda s, c, pt: (layer, pt[s * n_pages + c * pps + i], 0, 0))

    const = lambda shape: pl.BlockSpec(shape, lambda s, c, pt: (0,) * len(shape))
    in_specs = [pl.BlockSpec((heads, None, t_new, qdec.shape[3]), lambda s, c, pt: (0, s, 0, 0)),
                const(wukt.shape), const(gkr.shape),
                pl.BlockSpec((rope_dim // 2, chunk_keys), lambda s, c, pt: (0, c)),
                pl.BlockSpec((rope_dim // 2, chunk_keys), lambda s, c, pt: (0, c)),
                const(cos_n.shape), const(sin_n.shape),
                pl.BlockSpec((None, PAGE_SIZE, kv_lora), lambda s, c, pt: (s, 0, 0)),
                pl.BlockSpec((None, rope_dim, PAGE_SIZE), lambda s, c, pt: (s, 0, 0))]
    in_specs += [page_spec((PAGE_SIZE, kv_lora), i) for i in range(pps)]
    in_specs += [page_spec((rope_dim, PAGE_SIZE), i) for i in range(pps)]
    grid_spec = pltpu.PrefetchScalarGridSpec(
        num_scalar_prefetch=1, grid=(n, nc), in_specs=in_specs,
        out_specs=pl.BlockSpec((None, rows, kv_lora), lambda s, c, pt: (s, 0, 0)),
        scratch_shapes=[pltpu.VMEM((DEC_STREAMS, rows, 1), F32), pltpu.VMEM((DEC_STREAMS, rows, 1), F32),
                        pltpu.VMEM((DEC_STREAMS, rows, kv_lora), F32),
                        pltpu.VMEM((DEC_STREAMS, heads, stream_keys), F32),
                        pltpu.VMEM((DEC_STREAMS, rows, stream_keys), F32)])
    return pl.pallas_call(
        functools.partial(_mla_dec_body, heads=heads, t_new=t_new, qk_dim=qk_dim, nope=nope),
        grid_spec=grid_spec,
        out_shape=jax.ShapeDtypeStruct((n, rows, kv_lora), F32),
        compiler_params=_params(2),
        name="mla_dec",
    )(page_table.reshape(-1), qdec, wukt, gkr, cos_t, sin_t, cos_n, sin_n, lat_new, kpe_new_t,
      *([cache_lat] * pps), *([cache_kpe_t] * pps))


def _log_sigmoid(z):
    return jnp.minimum(z, 0.0) - jnp.log1p(jnp.exp(-jnp.abs(z)))


def _fox_pre_body(x_ref, gmix_ref, wq_ref, wk_ref, wv_ref, wf_ref, bf_ref, gq_ref, gk_ref, tri_ref, pq_ref, pk_ref,
                  cq_ref, ck_ref, vone_ref, qa_out, ka_out, vb_out, kf_out, vf_out, lf_out, carry_sc,
                  *, heads, kv_heads, head_dim, tiles_per_seq):
    i = pl.program_id(0)

    @pl.when(i % tiles_per_seq == 0)
    def _():
        carry_sc[...] = jnp.zeros_like(carry_sc)

    h = _rms(x_ref[...], gmix_ref[...]).astype(BF16)
    tm = h.shape[0]
    lane = lax.broadcasted_iota(jnp.int32, (1, LANES), 1)
    lf = jnp.where(lane < heads, _log_sigmoid(_dot(h, wf_ref[...]) + bf_ref[...]), 0.0)
    lf_out[...] = lf
    cum = carry_sc[...]
    for piece in _split(lf):
        cum = cum + _dot(tri_ref[...], piece)
    carry_sc[...] = cum[tm - 1:tm, :]
    cum_pieces = _split(cum * LOG2E)

    v = _dot(h, wv_ref[...])
    vf_out[...] = v
    vb_out[...] = (v + vone_ref[...]).astype(BF16)
    k = _dot(h, wk_ref[...])
    k_aug = ck_ref[...]
    for j, piece in enumerate(cum_pieces):
        k_aug = k_aug + _dot(piece, pk_ref[j])
    for n in range(kv_heads):
        sl = slice(n * LANES, (n + 1) * LANES)
        kn = _rms(k[:, sl], gk_ref[...], head_dim)
        kf_out[:, sl] = kn
        ka_out[:, sl] = (kn + k_aug[:, sl]).astype(BF16)
    scale = head_dim ** -0.5 * LOG2E
    for p in range(heads // 2):
        cols = slice(p * 2 * LANES, (p + 1) * 2 * LANES)
        q2 = _dot(h, wq_ref[:, cols])
        q_aug = cq_ref[:, cols]
        for j, piece in enumerate(cum_pieces):
            q_aug = q_aug + _dot(piece, pq_ref[j, :, cols])
        for e in range(2):
            sl = slice(e * LANES, (e + 1) * LANES)
            out = slice((2 * p + e) * LANES, (2 * p + e + 1) * LANES)
            qa_out[:, out] = (_rms(q2[:, sl], gq_ref[...], head_dim) * scale + q_aug[:, sl]).astype(BF16)


def _fox_pre(x, gmix, w, seq_rows, heads, kv_heads, head_dim):
    r, d = x.shape
    tm = _tile(math.gcd(seq_rows, r), (512, 256, 128, 64, 32, 16, 8))
    qw, kw = heads * LANES, kv_heads * LANES
    row = lambda c: pl.BlockSpec((tm, c), lambda i: (i, 0))
    tri = (lax.broadcasted_iota(jnp.int32, (tm, tm), 0) >= lax.broadcasted_iota(jnp.int32, (tm, tm), 1)).astype(BF16)
    ins = [x, gmix, w["w_q"], w["w_k"], w["w_v"], w["w_f"], w["b_f"], w["g_q"], w["g_k"], tri,
           w["place_q"], w["place_k"], w["const_q"], w["const_k"], w["v_one"]]
    return pl.pallas_call(
        functools.partial(_fox_pre_body, heads=heads, kv_heads=kv_heads, head_dim=head_dim,
                          tiles_per_seq=seq_rows // tm),
        grid=(r // tm,),
        in_specs=[row(d)] + [_full(a.shape) for a in ins[1:]],
        out_specs=[row(qw), row(kw), row(kw), row(kw), row(kw), row(LANES)],
        out_shape=[jax.ShapeDtypeStruct((r, qw), BF16), jax.ShapeDtypeStruct((r, kw), BF16),
                   jax.ShapeDtypeStruct((r, kw), BF16), jax.ShapeDtypeStruct((r, kw), F32),
                   jax.ShapeDtypeStruct((r, kw), F32), jax.ShapeDtypeStruct((r, LANES), F32)],
        scratch_shapes=[pltpu.VMEM((1, LANES), F32)],
        compiler_params=_params(1),
        name="fox_pre",
    )(*ins)


def _fox_dec_body(pt_ref, q_ref, lfr_ref, kn_ref, vn_ref, wsuf_ref, tblk_ref, ones_blk_ref, *rest, heads, t_new):
    pps = PAGES_PER_STEP
    k_refs, v_refs, lf_refs = rest[:pps], rest[pps:2 * pps], rest[2 * pps:3 * pps]
    o_ref, m_sc, l_sc, acc_sc, carry_sc, g_sc, s_sc, fn_sc = rest[3 * pps:]
    c = pl.program_id(1)
    rows = heads * t_new

    @pl.when(c == 0)
    def _():
        _softmax_init(m_sc, l_sc, acc_sc)
        carry_sc[...] = jnp.zeros_like(carry_sc)
        fn = jnp.zeros((rows, LANES), F32)
        for piece in _split(lfr_ref[...]):
            fn = fn + _dot(tblk_ref[...], piece)
        fn_sc[...] = fn

    q = q_ref[...]
    fn = fn_sc[...]

    lf = jnp.concatenate([r[...] for r in lf_refs], axis=0)
    suf = jnp.zeros((pps * heads, 2 * PAGE_SIZE), F32)
    for piece in _split(lf):
        suf = suf + _dot(piece, wsuf_ref[...])
    running = carry_sc[...]
    for i in reversed(range(pps)):
        rsl = slice(i * heads, (i + 1) * heads)
        g_sc[:, i * PAGE_SIZE:(i + 1) * PAGE_SIZE] = (suf[rsl, :PAGE_SIZE] + running) * LOG2E
        running = running + suf[rsl, PAGE_SIZE:]
    carry_sc[...] = running

    per = pps // DEC_STREAMS
    width = per * PAGE_SIZE
    for st in range(DEC_STREAMS):
        pages = range(st * per, (st + 1) * per)
        kt = jnp.concatenate([k_refs[i][...] for i in pages], axis=1).astype(BF16)
        vt = jnp.concatenate([v_refs[i][...] for i in pages], axis=1).astype(BF16)
        s = _dot(q, kt)
        for h in range(heads):
            sl = slice(h * t_new, (h + 1) * t_new)
            s_sc[st, sl, :] = s[sl] + g_sc[h:h + 1, st * width:(st + 1) * width] + fn[sl, :1] * LOG2E
        _softmax_update(s_sc[st], vt, m_sc, l_sc, acc_sc, st, v_transposed=True)

    @pl.when(c == pl.num_programs(1) - 1)
    def _():
        rid = lax.broadcasted_iota(jnp.int32, (rows, LANES), 0) % t_new
        cid = lax.broadcasted_iota(jnp.int32, (rows, LANES), 1)
        diag = jnp.where(rid == cid, fn, 0.0)
        fcol = jnp.zeros((rows, LANES), F32)
        for piece in _split(diag):
            fcol = fcol + _dot(ones_blk_ref[...], piece)
        sn = _dot(q, kn_ref[...]) + (fn - fcol) * LOG2E
        sn = jnp.where(cid <= rid, sn, NEG_INF)
        _softmax_update(sn, vn_ref[...], m_sc, l_sc, acc_sc, 0, v_transposed=True)
        o_ref[...] = _softmax_merge(m_sc, l_sc, acc_sc)


def _fox_dec(page_table, q_bd, lf_rows, k_new_t, v_new_t, cache_kt, cache_vt, cache_lft, layer, heads, t_new):
    n, n_pages = page_table.shape
    pps = PAGES_PER_STEP
    assert n_pages % pps == 0 and pps % DEC_STREAMS == 0
    nc = n_pages // pps
    rows = heads * t_new
    width = cache_kt.shape[2]
    chunk_keys = pps * PAGE_SIZE
    assert rows == LANES

    jj = lax.broadcasted_iota(jnp.int32, (PAGE_SIZE, 2 * PAGE_SIZE), 0)
    ss = lax.broadcasted_iota(jnp.int32, (PAGE_SIZE, 2 * PAGE_SIZE), 1)
    wsuf = ((jj > ss) | (ss >= PAGE_SIZE)).astype(BF16)
    ra = lax.broadcasted_iota(jnp.int32, (rows, rows), 0)
    rb = lax.broadcasted_iota(jnp.int32, (rows, rows), 1)
    same_head = (ra // t_new) == (rb // t_new)
    tblk = (same_head & (rb % t_new <= ra % t_new)).astype(BF16)
    ones_blk = same_head.astype(BF16)

    def page_spec(arr, i):
        return pl.BlockSpec((None, None) + arr.shape[2:],
                            lambda s, c, pt: (layer, pt[s * n_pages + (nc - 1 - c) * pps + i], 0, 0))

    const = lambda shape: pl.BlockSpec(shape, lambda s, c, pt: (0,) * len(shape))
    per_seq = lambda a: pl.BlockSpec((None,) + a.shape[1:], lambda s, c, pt: (s, 0, 0))
    in_specs = [per_seq(q_bd), per_seq(lf_rows), per_seq(k_new_t), per_seq(v_new_t),
                const(wsuf.shape), const(tblk.shape), const(ones_blk.shape)]
    in_specs += [page_spec(cache_kt, i) for i in range(pps)] + [page_spec(cache_vt, i) for i in range(pps)]
    in_specs += [page_spec(cache_lft, i) for i in range(pps)]
    grid_spec = pltpu.PrefetchScalarGridSpec(
        num_scalar_prefetch=1, grid=(n, nc), in_specs=in_specs,
        out_specs=pl.BlockSpec((None, rows, width), lambda s, c, pt: (s, 0, 0)),
        scratch_shapes=[pltpu.VMEM((DEC_STREAMS, rows, 1), F32), pltpu.VMEM((DEC_STREAMS, rows, 1), F32),
                        pltpu.VMEM((DEC_STREAMS, rows, width), F32),
                        pltpu.VMEM((heads, PAGE_SIZE), F32), pltpu.VMEM((heads, chunk_keys), F32),
                        pltpu.VMEM((DEC_STREAMS, rows, chunk_keys // DEC_STREAMS), F32),
                        pltpu.VMEM((rows, LANES), F32)])
    return pl.pallas_call(
        functools.partial(_fox_dec_body, heads=heads, t_new=t_new),
        grid_spec=grid_spec,
        out_shape=jax.ShapeDtypeStruct((n, rows, width), F32),
        compiler_params=_params(2),
        name="fox_dec",
    )(page_table.reshape(-1), q_bd, lf_rows, k_new_t, v_new_t, wsuf, tblk, ones_blk,
      *([cache_kt] * pps), *([cache_vt] * pps), *([cache_lft] * pps))


def _conv_glu_body(x_ref, gmix_ref, w_ref, b_ref, u_out):
    h = _rms(x_ref[...], gmix_ref[...]).astype(BF16)
    ch = u_out.shape[1]
    step = 2 * LANES
    for c in range(ch // step):
        a = _dot(h, w_ref[:, c * step:(c + 1) * step]) + b_ref[:, c * step:(c + 1) * step]
        gate = _dot(h, w_ref[:, ch + c * step:ch + (c + 1) * step]) + b_ref[:, ch + c * step:ch + (c + 1) * step]
        u_out[:, c * step:(c + 1) * step] = a * jax.nn.sigmoid(gate)


def _conv_glu(x, gmix, w, b):
    r, d = x.shape
    ch = w.shape[1] // 2
    tm = _tile(r, (512, 256, 128, 64, 32, 16, 8))
    return pl.pallas_call(
        _conv_glu_body,
        grid=(r // tm,),
        in_specs=[pl.BlockSpec((tm, d), lambda i: (i, 0)), _full((1, d)), _full(w.shape), _full((1, 2 * ch))],
        out_specs=pl.BlockSpec((tm, ch), lambda i: (i, 0)),
        out_shape=jax.ShapeDtypeStruct((r, ch), F32),
        compiler_params=_params(1),
        name="conv_glu",
    )(x, gmix, w.astype(BF16), b.reshape(1, -1))


def _ln_silu(z, g, b):
    mu = jnp.mean(z, axis=-1, keepdims=True)
    zc = z - mu
    y = zc * lax.rsqrt(jnp.mean(zc * zc, axis=-1, keepdims=True) + NORM_EPS) * g + b
    return y * jax.nn.sigmoid(y)


HALO = 32


def _conv_dw_prompt_body(u_ref, halo_ref, w_ref, bdw_ref, g_ref, b_ref, z_out, ext_sc, *, width, tiles_per_seq):
    i = pl.program_id(0)
    tm = u_ref.shape[0]
    first = i % tiles_per_seq == 0
    ext_sc[:HALO, :] = jnp.where(first, 0.0, halo_ref[...])
    ext_sc[HALO:, :] = u_ref[...]
    acc = jnp.zeros(u_ref.shape, F32) + bdw_ref[...]
    off = HALO - (width - 1)
    for w in range(width):
        acc = acc + ext_sc[off + w:off + w + tm, :] * w_ref[w:w + 1, :]
    z_out[...] = _ln_silu(acc, g_ref[...], b_ref[...]).astype(BF16)


def _conv_dw_prompt(u, w_dw, b_dw, g_ln, b_ln, seq_rows):
    r, ch = u.shape
    width = w_dw.shape[0]
    assert width - 1 <= HALO
    tm = _tile(seq_rows, (256, 128, 64, 32))
    ratio = tm // HALO
    vec = lambda a: a.reshape(1, ch)
    return pl.pallas_call(
        functools.partial(_conv_dw_prompt_body, width=width, tiles_per_seq=seq_rows // tm),
        grid=(r // tm,),
        in_specs=[pl.BlockSpec((tm, ch), lambda i: (i, 0)),
                  pl.BlockSpec((HALO, ch), lambda i: (jnp.maximum(i * ratio - 1, 0), 0)),
                  _full((width, ch)), _full((1, ch)), _full((1, ch)), _full((1, ch))],
        out_specs=pl.BlockSpec((tm, ch), lambda i: (i, 0)),
        out_shape=jax.ShapeDtypeStruct((r, ch), BF16),
        scratch_shapes=[pltpu.VMEM((tm + HALO, ch), F32)],
        compiler_params=_params(1),
        name="conv_dw_prompt",
    )(u, u, w_dw, vec(b_dw), vec(g_ln), vec(b_ln))


def _conv_dw_sample_body(ext_ref, w_ref, bdw_ref, g_ref, b_ref, z_out, *, width, t_new):
    acc = jnp.zeros(z_out.shape, F32) + bdw_ref[...]
    for w in range(width):
        acc = acc + ext_ref[:, w:w + t_new, :] * w_ref[w:w + 1, :]
    z_out[...] = _ln_silu(acc, g_ref[...], b_ref[...]).astype(BF16)


def _conv_dw_sample(ext, w_dw, b_dw, g_ln, b_ln, t_new):
    n, rows, ch = ext.shape
    width = w_dw.shape[0]
    nb = _tile(n, (16, 8, 4, 2, 1))
    vec = lambda a: a.reshape(1, ch)
    return pl.pallas_call(
        functools.partial(_conv_dw_sample_body, width=width, t_new=t_new),
        grid=(n // nb,),
        in_specs=[pl.BlockSpec((nb, rows, ch), lambda i: (i, 0, 0)),
                  _full((width, ch)), _full((1, ch)), _full((1, ch)), _full((1, ch))],
        out_specs=pl.BlockSpec((nb, t_new, ch), lambda i: (i, 0, 0)),
        out_shape=jax.ShapeDtypeStruct((n, t_new, ch), BF16),
        compiler_params=_params(1),
        name="conv_dw_sample",
    )(ext, w_dw, vec(b_dw), vec(g_ln), vec(b_ln))


def _pad_last(a, n):
    return jnp.pad(a, [(0, 0)] * (a.ndim - 1) + [(0, n - a.shape[-1])])


def _rope_tables(pos, half):
    inv = ROPE_BASE ** (-jnp.arange(half, dtype=F32) / half)
    ang = pos.astype(F32)[:, None] * inv[None, :]
    return jnp.cos(ang), jnp.sin(ang)


def _mla_layer(x, gmix, pos_rows, dims, cache_lat, cache_kpe, layer, page_table,
               w_dq, g_qa, w_uq, g_qn, w_dkv, g_kva, w_ukv, g_kn, w_o):
    batch, seq, n, t_new = dims
    d = x.shape[1]
    heads, qk_dim = w_uq.shape[1], w_uq.shape[2]
    kv_lora = w_ukv.shape[0]
    nope = w_ukv.shape[2] - w_o.shape[1]
    v_dim = w_o.shape[1]
    rope_dim = qk_dim - nope
    half = rope_dim // 2
    assert nope + rope_dim <= LANES and nope == 64 and rope_dim == 32 and v_dim < LANES
    bs = batch * seq
    past = page_table.shape[1] * PAGE_SIZE

    w_dkv_pad = jnp.concatenate([w_dkv[:, :kv_lora], jnp.zeros((d, nope), F32), w_dkv[:, kv_lora:],
                                 jnp.zeros((d, LANES - qk_dim), F32)], axis=1)
    gslot = lambda g: _pad_last(g.reshape(1, qk_dim), LANES)
    w = dict(
        w_dq=w_dq.astype(BF16), g_qa=g_qa.reshape(1, -1),
        w_uq=_pad_last(w_uq, LANES).reshape(w_uq.shape[0], heads * LANES).astype(BF16), g_q=gslot(g_qn),
        w_dkv=w_dkv_pad.astype(BF16), g_kva=g_kva.reshape(1, -1),
        w_uk=_pad_last(w_ukv[:, :, :nope], LANES).reshape(kv_lora, heads * LANES).astype(BF16),
        w_uv=_pad_last(w_ukv[:, :, nope:], LANES).reshape(kv_lora, heads * LANES).astype(BF16), g_k=gslot(g_kn),
        v_one=_sum_lane_row(heads, v_dim))
    shift = _score_shift(g_qn, g_kn, qk_dim, qk_dim ** -0.5 * LOG2E)
    lane_hot = _sum_lane_row(1, qk_dim)
    w.update(q_one=lane_hot, k_shift=-shift * lane_hot)
    cos, sin = _rope_tables(pos_rows, half)
    r = x.shape[0]
    cos_slot = jnp.concatenate([jnp.ones((r, nope), F32), cos, cos, jnp.ones((r, LANES - qk_dim), F32)], axis=1)
    sin_slot = jnp.concatenate([jnp.zeros((r, nope), F32), -sin, sin, jnp.zeros((r, LANES - qk_dim), F32)], axis=1)
    q, k, v, lat, kpe_slot = _mla_pre(x, gmix.reshape(1, d), w, cos_slot, sin_slot, heads, qk_dim)
    kpe = kpe_slot[:, nope:qk_dim]

    o_p = _flash(q[:bs], k[:bs], v[:bs], shift, batch, seq, heads, 1, 2, v_dim)

    wukt = jnp.transpose(w_ukv[:, :, :nope], (1, 2, 0))
    wcat = jnp.zeros((heads, LANES, kv_lora + LANES), F32)
    wcat = wcat.at[:, :nope, :kv_lora].set(wukt)
    wcat = wcat.at[:, nope:qk_dim, kv_lora:kv_lora + rope_dim].set(jnp.eye(rope_dim, dtype=F32))
    qdec = _mla_qdec(q[bs:], gslot(g_kn), wcat.astype(BF16)).reshape(heads, n, t_new, kv_lora + LANES)
    cos_t, sin_t = (a.T for a in _rope_tables(jnp.arange(past), half))
    cos_n, sin_n = (a.T for a in _rope_tables(past + jnp.arange(PAGE_SIZE), half))
    new_rows = lambda a: jnp.pad(a.reshape(n, t_new, -1), ((0, 0), (0, PAGE_SIZE - t_new), (0, 0)))
    o_lat = _mla_dec(page_table, qdec, wukt.reshape(heads * nope, kv_lora).astype(BF16),
                     g_kn[nope:].reshape(rope_dim, 1), cos_t, sin_t, cos_n, sin_n,
                     new_rows(lat[bs:]).astype(BF16), jnp.transpose(new_rows(kpe[bs:]), (0, 2, 1)),
                     cache_lat, jnp.transpose(cache_kpe, (0, 1, 3, 2)), layer, heads, qk_dim, nope)
    w_uv_dec = _pad_last(jnp.transpose(w_ukv[:, :, nope:], (1, 0, 2)), LANES).astype(BF16)
    o_s = _mla_unabsorb(o_lat.reshape(n, heads, t_new, kv_lora), w_uv_dec)

    w_o_pad = jnp.pad(w_o, ((0, 0), (0, LANES - v_dim), (0, 0))).reshape(heads * LANES, d)
    return (jnp.concatenate([o_p, o_s], axis=0), w_o_pad, jnp.zeros((d,), F32)), lat, kpe


def _fox_layer(x, gmix, dims, cache_k, cache_v, cache_lf, layer, page_table,
               w_q, w_k, w_v, w_f, b_f, g_qn, g_kn, w_o):
    batch, seq, n, t_new = dims
    d = x.shape[1]
    heads, head_dim = w_q.shape[1], w_q.shape[2]
    kv_heads = w_k.shape[1]
    group = heads // kv_heads
    bs = batch * seq
    sp = FORGET_SPLIT
    base = head_dim + 1
    shift_lane = base + sp * (group + 1)
    assert shift_lane < LANES and heads <= LANES
    shift = _score_shift(g_qn, g_kn, head_dim, head_dim ** -0.5 * LOG2E)

    place_q = jnp.zeros((sp, LANES, heads * LANES), F32)
    place_k = jnp.zeros((sp, LANES, kv_heads * LANES), F32)
    const_q = jnp.zeros((1, heads * LANES), F32)
    const_k = jnp.zeros((1, kv_heads * LANES), F32)
    hh = jnp.arange(heads)
    for j in range(sp):
        place_q = place_q.at[j, hh, hh * LANES + base + sp * group + j].set(1.0)
        place_k = place_k.at[j, hh, (hh // group) * LANES + base + sp * (hh % group) + j].set(-1.0)
        const_q = const_q.at[0, hh * LANES + base + sp * (hh % group) + j].set(1.0)
        const_k = const_k.at[0, jnp.arange(kv_heads) * LANES + base + sp * group + j].set(1.0)
    const_q = const_q.at[0, hh * LANES + shift_lane].set(1.0)
    const_k = const_k.at[0, jnp.arange(kv_heads) * LANES + shift_lane].set(-shift)
    gslot = lambda g: _pad_last(g.reshape(1, head_dim), LANES)
    w = dict(w_q=_pad_last(w_q, LANES).reshape(d, heads * LANES).astype(BF16),
             w_k=_pad_last(w_k, LANES).reshape(d, kv_heads * LANES).astype(BF16),
             w_v=_pad_last(w_v, LANES).reshape(d, kv_heads * LANES).astype(BF16),
             w_f=_pad_last(w_f, LANES).astype(BF16), b_f=_pad_last(b_f.reshape(1, heads), LANES),
             g_q=gslot(g_qn), g_k=gslot(g_kn), place_q=place_q.astype(BF16), place_k=place_k.astype(BF16),
             const_q=const_q, const_k=const_k, v_one=_sum_lane_row(kv_heads, head_dim))
    q_aug, k_aug, v_b, k_f, v_f, lf = _fox_pre(x, gmix.reshape(1, d), w, seq, heads, kv_heads, head_dim)
    unslot = lambda a, nh: a.reshape(a.shape[0], nh, LANES)[:, :, :head_dim]
    k_out, v_out, lf_out = unslot(k_f, kv_heads), unslot(v_f, kv_heads), lf[:, :heads]

    o_p = _flash(q_aug[:bs], k_aug[:bs], v_b[:bs], shift, batch, seq, heads, group, group, head_dim)

    kvw = kv_heads * head_dim
    q_s = unslot(q_aug[bs:], heads).reshape(n, t_new, kv_heads, group, head_dim)
    q_bd = jnp.einsum("ntkgd,kj->nkgtjd", q_s.astype(F32), jnp.eye(kv_heads, dtype=F32))
    q_bd = q_bd.reshape(n, heads * t_new, kvw).astype(BF16)
    lf_rows = jnp.broadcast_to(jnp.transpose(lf_out[bs:].reshape(n, t_new, heads), (0, 2, 1))[..., None],
                               (n, heads, t_new, LANES)).reshape(n, heads * t_new, LANES)
    new_t = lambda a: jnp.transpose(jnp.pad(a.reshape(n, t_new, kvw), ((0, 0), (0, PAGE_SIZE - t_new), (0, 0))),
                                    (0, 2, 1)).astype(BF16)
    pool = cache_k.shape[1]
    page_t = lambda c: jnp.transpose(c, (0, 1, 3, 4, 2)).reshape(c.shape[0], pool, kvw, PAGE_SIZE)
    o_bd = _fox_dec(page_table, q_bd, lf_rows, new_t(k_out[bs:]), new_t(v_out[bs:]), page_t(cache_k), page_t(cache_v),
                    jnp.transpose(cache_lf, (0, 1, 3, 2)), layer, heads, t_new)
    o_s = jnp.einsum("nkgtjd,kj->ntkgd", o_bd.reshape(n, kv_heads, group, t_new, kv_heads, head_dim),
                     jnp.eye(kv_heads, dtype=F32))
    o_s = _pad_last(o_s.reshape(n * t_new, heads, head_dim), LANES).reshape(n * t_new, heads * LANES).astype(BF16)

    w_o_pad = jnp.pad(w_o, ((0, 0), (0, LANES - head_dim), (0, 0))).reshape(heads * LANES, d)
    return (jnp.concatenate([o_p, o_s], axis=0), w_o_pad, jnp.zeros((d,), F32)), k_out, v_out, lf_out


def _conv_layer(x, gmix, dims, state, w_pw1, b_pw1, w_dw, b_dw, g_ln, b_ln, w_pw2, b_pw2):
    batch, seq, n, t_new = dims
    d = x.shape[1]
    bs = batch * seq
    keep = w_dw.shape[0] - 1
    assert seq >= keep
    u = _conv_glu(x, gmix.reshape(1, d), w_pw1, b_pw1)
    ch = u.shape[1]
    z_p = _conv_dw_prompt(u[:bs], w_dw, b_dw, g_ln, b_ln, seq)
    ext_s = jnp.concatenate([state.astype(F32), u[bs:].reshape(n, t_new, ch)], axis=1)
    rows = -(-(keep + t_new) // 8) * 8
    z_s = _conv_dw_sample(jnp.pad(ext_s, ((0, 0), (0, rows - keep - t_new), (0, 0))), w_dw, b_dw, g_ln, b_ln, t_new)
    z = jnp.concatenate([z_p, z_s.reshape(n * t_new, ch)], axis=0)
    return (z, w_pw2, b_pw2), u[:bs].reshape(batch, seq, ch)[:, seq - keep:], ext_s[:, t_new:]


def kernel(x_prompt, x_sample, cache_mla_latent, cache_mla_krope, cache_fox_k, cache_fox_v, cache_fox_logf, state_conv, page_table, norm_mix, norm_mlp, mlp_up, mlp_down, mla_w_dq, mla_g_qa, mla_w_uq, mla_g_qn, mla_w_dkv, mla_g_kva, mla_w_ukv, mla_g_kn, mla_w_o, fox_w_q, fox_w_k, fox_w_v, fox_w_f, fox_b_f, fox_g_qn, fox_g_kn, fox_w_o, conv_w_pw1, conv_b_pw1, conv_w_dw, conv_b_dw, conv_g_ln, conv_b_ln, conv_w_pw2, conv_b_pw2):
    batch, seq, d = x_prompt.shape
    n, t_new, _ = x_sample.shape
    dims = (batch, seq, n, t_new)
    bs = batch * seq
    past = page_table.shape[1] * PAGE_SIZE
    depth = norm_mix.shape[0]
    x = jnp.concatenate([x_prompt.reshape(bs, d), x_sample.reshape(n * t_new, d)], axis=0)
    pos_rows = jnp.concatenate([jnp.tile(jnp.arange(seq), batch), jnp.tile(past + jnp.arange(t_new), n)])

    outs = {name: [] for name in ("lat", "kpe", "fk", "fv", "flf", "cp", "cs")}
    for i in range(depth):
        kind, j = i % 3, i // 3
        if kind == 0:
            mix, lat, kpe = _mla_layer(x, norm_mix[i], pos_rows, dims, cache_mla_latent, cache_mla_krope, j, page_table,
                                       mla_w_dq[j], mla_g_qa[j], mla_w_uq[j], mla_g_qn[j], mla_w_dkv[j], mla_g_kva[j],
                                       mla_w_ukv[j], mla_g_kn[j], mla_w_o[j])
            outs["lat"].append(lat)
            outs["kpe"].append(kpe)
        elif kind == 1:
            mix, fk, fv, flf = _fox_layer(x, norm_mix[i], dims, cache_fox_k, cache_fox_v, cache_fox_logf, j, page_table,
                                          fox_w_q[j], fox_w_k[j], fox_w_v[j], fox_w_f[j], fox_b_f[j], fox_g_qn[j],
                                          fox_g_kn[j], fox_w_o[j])
            outs["fk"].append(fk)
            outs["fv"].append(fv)
            outs["flf"].append(flf)
        else:
            mix, cp, cs = _conv_layer(x, norm_mix[i], dims, state_conv[j], conv_w_pw1[j], conv_b_pw1[j], conv_w_dw[j],
                                      conv_b_dw[j], conv_g_ln[j], conv_b_ln[j], conv_w_pw2[j], conv_b_pw2[j])
            outs["cp"].append(cp)
            outs["cs"].append(cs)
        x = _mlp(x, *mix, norm_mlp[i], mlp_up[i], mlp_down[i])

    def split_rows(name):
        stacked = jnp.stack(outs[name])
        tail = stacked.shape[2:]
        return (stacked[:, :bs].reshape((-1, batch, seq) + tail), stacked[:, bs:].reshape((-1, n, t_new) + tail))

    lat_p, lat_s = split_rows("lat")
    kpe_p, kpe_s = split_rows("kpe")
    fk_p, fk_s = split_rows("fk")
    fv_p, fv_s = split_rows("fv")
    flf_p, flf_s = split_rows("flf")
    return (x[:bs].reshape(batch, seq, d), x[bs:].reshape(n, t_new, d), lat_p, lat_s, kpe_p, kpe_s,
            fk_p, fk_s, fv_p, fv_s, flf_p, flf_s, jnp.stack(outs["cp"]), jnp.stack(outs["cs"]))
```

```python
import functools
import math

import jax
import jax.numpy as jnp
from jax import lax
from jax.experimental import pallas as pl
from jax.experimental.pallas import tpu as pltpu

F32 = jnp.float32
BF16 = jnp.bfloat16

NORM_EPS = 1e-6
NEG_INF = -1e30
ROPE_BASE = 10000.0
LOG2E = math.log2(math.e)
FORGET_SPLIT = 3
LANES = 128
PAGE_SIZE = 128
PAGES_PER_STEP = 16
DEC_STREAMS = 2
VMEM_LIMIT = 56 * 1024 * 1024

NT_DIMS = (((1,), (1,)), ((), ()))


def _params(n_axes):
    return pltpu.CompilerParams(dimension_semantics=("arbitrary",) * n_axes, vmem_limit_bytes=VMEM_LIMIT)


def _tile(n, cands):
    for c in cands:
        if n % c == 0:
            return c
    raise ValueError(f"no tile for {n}")


def _dot(a, b):
    return jnp.dot(a, b, preferred_element_type=F32)


def _dot_nt(a, b):
    return lax.dot_general(a, b, NT_DIMS, preferred_element_type=F32)


def _rms(x, g, n=None):
    n = x.shape[-1] if n is None else n
    ss = jnp.sum(x * x, axis=-1, keepdims=True)
    return x * lax.rsqrt(ss * (1.0 / n) + NORM_EPS) * g


def _split(x, pieces=FORGET_SPLIT):
    out = []
    for _ in range(pieces):
        p = x.astype(BF16)
        out.append(p)
        x = x - p.astype(F32)
    return out


def _full(shape):
    nd = len(shape)
    return pl.BlockSpec(shape, lambda *_: (0,) * nd)


def _mlp_body(x_ref, ap_ref, as_ref, wa_ref, ba_ref, g_ref, wu_ref, wd_ref, o_ref, x1_sc, h_sc, acc_sc, *, prompt_tiles):
    i, f = pl.program_id(0), pl.program_id(1)

    def first(a_ref):
        x1 = x_ref[...] + _dot(a_ref[...], wa_ref[...]) + ba_ref[...]
        x1_sc[...] = x1
        h_sc[...] = _rms(x1, g_ref[...]).astype(BF16)
        acc_sc[...] = jnp.zeros_like(acc_sc)

    pl.when((f == 0) & (i < prompt_tiles))(lambda: first(ap_ref))
    pl.when((f == 0) & (i >= prompt_tiles))(lambda: first(as_ref))

    a = jnp.maximum(_dot(h_sc[...], wu_ref[...]), 0.0)
    acc_sc[...] += _dot((a * a).astype(BF16), wd_ref[...])

    @pl.when(f == pl.num_programs(1) - 1)
    def _():
        o_ref[...] = x1_sc[...] + acc_sc[...]


def _mlp(x, a_p, a_s, w_a, b_a, g, w_up, w_down):
    r, d = x.shape
    k = a_p.shape[1]
    dff = w_up.shape[1]
    bs = r - a_s.shape[0]
    tm = _tile(math.gcd(bs, a_s.shape[0]), (512, 256, 128, 64, 32, 16, 8))
    tf = _tile(dff, (1024, 512, 256, 128))
    pt = bs // tm
    return pl.pallas_call(
        functools.partial(_mlp_body, prompt_tiles=pt),
        grid=(r // tm, dff // tf),
        in_specs=[pl.BlockSpec((tm, d), lambda i, f: (i, 0)),
                  pl.BlockSpec((tm, k), lambda i, f: (jnp.minimum(i, pt - 1), 0)),
                  pl.BlockSpec((tm, k), lambda i, f: (jnp.maximum(i - pt, 0), 0)),
                  pl.BlockSpec((k, d), lambda i, f: (0, 0)),
                  pl.BlockSpec((1, d), lambda i, f: (0, 0)),
                  pl.BlockSpec((1, d), lambda i, f: (0, 0)),
                  pl.BlockSpec((d, tf), lambda i, f: (0, f)),
                  pl.BlockSpec((tf, d), lambda i, f: (f, 0))],
        out_specs=pl.BlockSpec((tm, d), lambda i, f: (i, 0)),
        out_shape=jax.ShapeDtypeStruct((r, d), F32),
        scratch_shapes=[pltpu.VMEM((tm, d), F32), pltpu.VMEM((tm, d), BF16), pltpu.VMEM((tm, d), F32)],
        compiler_params=_params(2),
        name="mlp",
    )(x, a_p, a_s, w_a.astype(BF16), b_a.reshape(1, d).astype(F32), g.reshape(1, d), w_up.astype(BF16),
      w_down.astype(BF16))


def _flash_body(q_ref, k_ref, v_ref, o_ref, m_sc, acc_sc, *, t, kv_of, sum_lane):
    qi = pl.program_id(2)
    streams = len(kv_of)
    m_sc[...] = jnp.full_like(m_sc, NEG_INF)
    acc_sc[...] = jnp.zeros_like(acc_sc)

    def step(j, masked):
        start = pl.multiple_of(j * t, t)
        for g in range(streams):
            kv = slice(kv_of[g] * LANES, (kv_of[g] + 1) * LANES)
            s = _dot_nt(q_ref[:, g * LANES:(g + 1) * LANES], k_ref[pl.ds(start, t), kv])
            if masked:
                row = lax.broadcasted_iota(jnp.int32, (t, t), 0)
                col = lax.broadcasted_iota(jnp.int32, (t, t), 1)
                s = jnp.where(col <= row, s, NEG_INF)
            m_prev = m_sc[g]
            m_new = jnp.maximum(m_prev, jnp.max(s, axis=-1, keepdims=True))
            p = jnp.exp2(s - m_new).astype(BF16)
            acc_sc[g] = jnp.exp2(m_prev - m_new) * acc_sc[g] + _dot(p, v_ref[pl.ds(start, t), kv])
            m_sc[g] = m_new

    def unmasked(j, carry):
        step(j, False)
        return carry

    lax.fori_loop(0, qi, unmasked, 0)
    step(qi, True)
    for g in range(streams):
        acc = acc_sc[g]
        o_ref[:, g * LANES:(g + 1) * LANES] = (acc / acc[:, sum_lane:sum_lane + 1]).astype(o_ref.dtype)


def _flash_shifted_body(q_ref, k_ref, v_ref, o_ref, acc_sc, *, t, kv_of, sum_lane):
    qi = pl.program_id(2)
    streams = len(kv_of)
    acc_sc[...] = jnp.zeros_like(acc_sc)

    def step(j, masked):
        start = pl.multiple_of(j * t, t)
        for g in range(streams):
            kv = slice(kv_of[g] * LANES, (kv_of[g] + 1) * LANES)
            s = _dot_nt(q_ref[:, g * LANES:(g + 1) * LANES], k_ref[pl.ds(start, t), kv])
            if masked:
                row = lax.broadcasted_iota(jnp.int32, (t, t), 0)
                col = lax.broadcasted_iota(jnp.int32, (t, t), 1)
                s = jnp.where(col <= row, s, NEG_INF)
            acc_sc[g] += _dot(jnp.exp2(s).astype(BF16), v_ref[pl.ds(start, t), kv])

    def unmasked(j, carry):
        step(j, False)
        return carry

    lax.fori_loop(0, qi, unmasked, 0)
    step(qi, True)
    for g in range(streams):
        acc = acc_sc[g]
        o_ref[:, g * LANES:(g + 1) * LANES] = (acc / acc[:, sum_lane:sum_lane + 1]).astype(o_ref.dtype)


MAX_SHIFT = 60.0


def _score_shift(g_q, g_k, n, scale):
    bound = n * jnp.max(jnp.abs(g_q)) * jnp.max(jnp.abs(g_k)) * scale
    return jnp.ceil(bound * 1.02 + 0.5)


def _flash(q, k, v, shift, batch, seq, heads, group, streams, sum_lane):
    t = _tile(seq, (512, 256, 128, 64, 32, 16))
    nq = seq // t
    assert heads % streams == 0 and (streams % group == 0 or group % streams == 0)
    if streams >= group:
        kv_w, kv_of = streams // group, tuple(g // group for g in range(streams))
        kv_idx = lambda hb: hb
    else:
        kv_w, kv_of = 1, (0,) * streams
        kv_idx = lambda hb: hb * streams // group

    def call(body, scratch, name):
        return pl.pallas_call(
            functools.partial(body, t=t, kv_of=kv_of, sum_lane=sum_lane),
            grid=(batch, heads // streams, nq),
            in_specs=[pl.BlockSpec((t, streams * LANES), lambda b, h, i: (b * nq + i, h)),
                      pl.BlockSpec((seq, kv_w * LANES), lambda b, h, i: (b, kv_idx(h))),
                      pl.BlockSpec((seq, kv_w * LANES), lambda b, h, i: (b, kv_idx(h)))],
            out_specs=pl.BlockSpec((t, streams * LANES), lambda b, h, i: (b * nq + i, h)),
            out_shape=jax.ShapeDtypeStruct((batch * seq, heads * LANES), BF16),
            scratch_shapes=scratch,
            compiler_params=_params(3),
            name=name,
        )

    acc = pltpu.VMEM((streams, t, LANES), F32)
    shifted = call(_flash_shifted_body, [acc], "flash_shifted")
    online = call(_flash_body, [pltpu.VMEM((streams, t, 1), F32), acc], "flash")
    return lax.cond(shift <= MAX_SHIFT, shifted, online, q, k, v)


def _sum_lane_row(heads, lane):
    return jnp.zeros((heads, LANES), F32).at[:, lane].set(1.0).reshape(1, heads * LANES)


def _rope_slot(x, cos, sin, lane):
    swapped = jnp.where(lane < 80, pltpu.roll(x, 112, 1), pltpu.roll(x, 16, 1))
    return x * cos + swapped * sin


def _mla_pre_body(x_ref, gmix_ref, wdq_ref, gqa_ref, wuq_ref, gq_ref, wdkv_ref, gkva_ref, wuk_ref, wuv_ref,
                  gk_ref, vone_ref, qone_ref, kshift_ref, cos_ref, sin_ref, q_out, k_out, v_out, lat_out, kpe_out,
                  *, heads, qk_dim):
    h = _rms(x_ref[...], gmix_ref[...]).astype(BF16)
    cq = _rms(_dot(h, wdq_ref[...]), gqa_ref[...]).astype(BF16)
    a = _dot(h, wdkv_ref[...])
    kv_lora = gkva_ref.shape[1]
    lat = _rms(a[:, :kv_lora], gkva_ref[...])
    lat_out[...] = lat
    kpe = a[:, kv_lora:]
    kpe_out[...] = kpe
    latb = lat.astype(BF16)
    cos, sin = cos_ref[...], sin_ref[...]
    lane = lax.broadcasted_iota(jnp.int32, (1, LANES), 1)
    scale = qk_dim ** -0.5 * LOG2E
    for p in range(heads // 2):
        cols = slice(p * 2 * LANES, (p + 1) * 2 * LANES)
        q2 = _dot(cq, wuq_ref[:, cols])
        k2 = _dot(latb, wuk_ref[:, cols])
        v_out[:, cols] = (_dot(latb, wuv_ref[:, cols]) + vone_ref[:, cols]).astype(BF16)
        for e in range(2):
            sl = slice(e * LANES, (e + 1) * LANES)
            out = slice((2 * p + e) * LANES, (2 * p + e + 1) * LANES)
            qh = _rms(q2[:, sl], gq_ref[...], qk_dim) * scale
            q_out[:, out] = (_rope_slot(qh, cos, sin, lane) + qone_ref[...]).astype(BF16)
            kh = _rms(k2[:, sl] + kpe, gk_ref[...], qk_dim)
            k_out[:, out] = (_rope_slot(kh, cos, sin, lane) + kshift_ref[...]).astype(BF16)


def _mla_pre(x, gmix, w, cos, sin, heads, qk_dim):
    r, d = x.shape
    tm = _tile(r, (512, 256, 128, 64, 32, 16, 8))
    hw = heads * LANES
    row = lambda c: pl.BlockSpec((tm, c), lambda i: (i, 0))
    ins = [x, gmix, w["w_dq"], w["g_qa"], w["w_uq"], w["g_q"], w["w_dkv"], w["g_kva"], w["w_uk"], w["w_uv"], w["g_k"],
           w["v_one"], w["q_one"], w["k_shift"]]
    return pl.pallas_call(
        functools.partial(_mla_pre_body, heads=heads, qk_dim=qk_dim),
        grid=(r // tm,),
        in_specs=[row(d)] + [_full(a.shape) for a in ins[1:]] + [row(LANES), row(LANES)],
        out_specs=[row(hw), row(hw), row(hw), row(w["g_kva"].shape[1]), row(LANES)],
        out_shape=[jax.ShapeDtypeStruct((r, hw), BF16)] * 3
        + [jax.ShapeDtypeStruct((r, w["g_kva"].shape[1]), F32), jax.ShapeDtypeStruct((r, LANES), F32)],
        compiler_params=_params(1),
        name="mla_pre",
    )(*ins, cos, sin)


def _mla_qdec_body(q_ref, gk_ref, w_ref, o_ref):
    o_ref[...] = _dot((q_ref[...].astype(F32) * gk_ref[...]).astype(BF16), w_ref[...])


def _mla_qdec(q_s, gk, wcat):
    nt = q_s.shape[0]
    heads, _, wd = wcat.shape
    return pl.pallas_call(
        _mla_qdec_body,
        grid=(heads,),
        in_specs=[pl.BlockSpec((nt, LANES), lambda h: (0, h)), _full((1, LANES)),
                  pl.BlockSpec((None, LANES, wd), lambda h: (h, 0, 0))],
        out_specs=pl.BlockSpec((None, nt, wd), lambda h: (h, 0, 0)),
        out_shape=jax.ShapeDtypeStruct((heads, nt, wd), F32),
        compiler_params=_params(1),
        name="mla_qdec",
    )(q_s, gk, wcat)


def _mla_unabsorb_body(o_ref, w_ref, out_ref):
    n, t, c = o_ref.shape
    out_ref[...] = _dot(o_ref[...].reshape(n * t, c).astype(BF16), w_ref[...]).astype(BF16)


def _mla_unabsorb(o_lat, w_uv):
    n, heads, t, c = o_lat.shape
    return pl.pallas_call(
        _mla_unabsorb_body,
        grid=(heads,),
        in_specs=[pl.BlockSpec((n, None, t, c), lambda h: (0, h, 0, 0)),
                  pl.BlockSpec((None, c, LANES), lambda h: (h, 0, 0))],
        out_specs=pl.BlockSpec((n * t, LANES), lambda h: (0, h)),
        out_shape=jax.ShapeDtypeStruct((n * t, heads * LANES), BF16),
        compiler_params=_params(1),
        name="mla_unabsorb",
    )(o_lat, w_uv)


def _softmax_update(s, v, m_sc, l_sc, acc_sc, st, v_transposed=False):
    m_prev = m_sc[st]
    m_new = jnp.maximum(m_prev, jnp.max(s, axis=-1, keepdims=True))
    alpha = jnp.exp2(m_prev - m_new)
    p = jnp.exp2(s - m_new)
    l_sc[st] = alpha * l_sc[st] + jnp.sum(p, axis=-1, keepdims=True)
    pv = _dot_nt(p.astype(BF16), v) if v_transposed else _dot(p.astype(BF16), v)
    acc_sc[st] = alpha * acc_sc[st] + pv
    m_sc[st] = m_new


def _softmax_init(m_sc, l_sc, acc_sc):
    m_sc[...] = jnp.full_like(m_sc, NEG_INF)
    l_sc[...] = jnp.zeros_like(l_sc)
    acc_sc[...] = jnp.zeros_like(acc_sc)


def _softmax_merge(m_sc, l_sc, acc_sc):
    m = m_sc[0]
    for st in range(1, DEC_STREAMS):
        m = jnp.maximum(m, m_sc[st])
    l = jnp.zeros_like(m)
    acc = jnp.zeros(acc_sc.shape[1:], F32)
    for st in range(DEC_STREAMS):
        w = jnp.exp2(m_sc[st] - m)
        l = l + w * l_sc[st]
        acc = acc + w * acc_sc[st]
    return acc / l


def _page_pipeline(pt_ref, base_of, hbm_refs, bufs, sem, layer):
    n_steps = pl.num_programs(0) * pl.num_programs(1)
    g = pl.program_id(0) * pl.num_programs(1) + pl.program_id(1)
    slot = lax.rem(g, 2)
    pps = bufs[0].shape[1]

    def copies(step, sl):
        out = []
        for i in range(pps):
            pid = 0 if step is None else pt_ref[base_of(step) + i]
            for a, (hbm, buf) in enumerate(zip(hbm_refs, bufs)):
                out.append(pltpu.make_async_copy(hbm.at[layer, pid], buf.at[sl, i], sem.at[a, sl]))
        return out

    @pl.when(g == 0)
    def _():
        for cp in copies(g, slot):
            cp.start()

    @pl.when(g + 1 < n_steps)
    def _():
        for cp in copies(g + 1, 1 - slot):
            cp.start()

    for cp in copies(None, slot):
        cp.wait()
    return slot


def _mla_dec_body(pt_ref, q_ref, wukt_ref, gkr_ref, cos_ref, sin_ref, cosn_ref, sinn_ref, latn_ref, kpen_ref,
                  lat_hbm, kpe_hbm, o_ref, lat_buf, kpe_buf, sem, m_sc, l_sc, acc_sc, rinv_sc, s_sc,
                  *, layer, heads, t_new, qk_dim, nope):
    pps = PAGES_PER_STEP
    c = pl.program_id(1)
    kv_lora = wukt_ref.shape[1]
    rows = heads * t_new
    slot = _page_pipeline(pt_ref, lambda step: step * pps, (lat_hbm, kpe_hbm), (lat_buf, kpe_buf), sem, layer)

    @pl.when(c == 0)
    def _():
        _softmax_init(m_sc, l_sc, acc_sc)

    q = q_ref[...].reshape(rows, q_ref.shape[-1]).astype(BF16)
    q_lat, q_rope = q[:, :kv_lora], q[:, kv_lora:kv_lora + LANES]
    rope_dim = qk_dim - nope
    half = rope_dim // 2

    def chunk(lat, kpt, cos, sin, st, new):
        n_keys = lat.shape[0]
        kt = _dot_nt(wukt_ref[...], lat)
        ss = jnp.sum((kt * kt).reshape(heads, nope, n_keys), axis=1)
        ss = ss + jnp.sum(kpt * kpt, axis=0, keepdims=True)
        rinv_sc[st, :, :n_keys] = lax.rsqrt(ss * (1.0 / qk_dim) + NORM_EPS)
        kg = kpt * gkr_ref[...]
        x1, x2 = kg[:half], kg[half:]
        krot = jnp.concatenate([x1 * cos - x2 * sin, x2 * cos + x1 * sin,
                                jnp.zeros((LANES - rope_dim, n_keys), F32)], axis=0).astype(BF16)
        s = _dot_nt(q_lat, lat) + _dot(q_rope, krot)
        for h in range(heads):
            sl = slice(h * t_new, (h + 1) * t_new)
            s_sc[st, sl, :n_keys] = s[sl] * rinv_sc[st, h:h + 1, :n_keys]
        s = s_sc[st, :, :n_keys]
        if new:
            tq = lax.broadcasted_iota(jnp.int32, (rows, n_keys), 0) % t_new
            tk = lax.broadcasted_iota(jnp.int32, (rows, n_keys), 1)
            s = jnp.where(tk <= tq, s, NEG_INF)
        _softmax_update(s, lat, m_sc, l_sc, acc_sc, st)

    per = pps // DEC_STREAMS
    for st in range(DEC_STREAMS):
        pages = range(st * per, (st + 1) * per)
        lat = jnp.concatenate([lat_buf[slot, i] for i in pages], axis=0).astype(BF16)
        kpt = jnp.concatenate([kpe_buf[slot, i] for i in pages], axis=1)
        tab = c * DEC_STREAMS + st
        chunk(lat, kpt, cos_ref[tab], sin_ref[tab], st, False)

    @pl.when(c == pl.num_programs(1) - 1)
    def _():
        chunk(latn_ref[...], kpen_ref[...], cosn_ref[...], sinn_ref[...], 0, True)
        o_ref[...] = _softmax_merge(m_sc, l_sc, acc_sc)


def _mla_dec(page_table, qdec, wukt, gkr, cos_t, sin_t, cos_n, sin_n, lat_new, kpe_new_t, cache_lat, cache_kpe_t,
             layer, heads, qk_dim, nope):
    n, n_pages = page_table.shape
    t_new = qdec.shape[2]
    kv_lora = wukt.shape[1]
    rope_dim = qk_dim - nope
    pps = PAGES_PER_STEP
    assert n_pages % pps == 0 and pps % DEC_STREAMS == 0 and t_new <= PAGE_SIZE
    nc = n_pages // pps
    rows = heads * t_new
    stream_keys = pps * PAGE_SIZE // DEC_STREAMS
    by_stream = lambda tab: jnp.transpose(tab.reshape(tab.shape[0], -1, stream_keys), (1, 0, 2))
    cos_t, sin_t = by_stream(cos_t), by_stream(sin_t)

    const = lambda shape: pl.BlockSpec(shape, lambda s, c, pt: (0,) * len(shape))
    hbm = pl.BlockSpec(memory_space=pl.ANY)
    in_specs = [pl.BlockSpec((heads, None, t_new, qdec.shape[3]), lambda s, c, pt: (0, s, 0, 0)),
                const(wukt.shape), const(gkr.shape), const(cos_t.shape), const(sin_t.shape),
                const(cos_n.shape), const(sin_n.shape),
                pl.BlockSpec((None, PAGE_SIZE, kv_lora), lambda s, c, pt: (s, 0, 0)),
                pl.BlockSpec((None, rope_dim, PAGE_SIZE), lambda s, c, pt: (s, 0, 0)),
                hbm, hbm]
    grid_spec = pltpu.PrefetchScalarGridSpec(
        num_scalar_prefetch=1, grid=(n, nc), in_specs=in_specs,
        out_specs=pl.BlockSpec((None, rows, kv_lora), lambda s, c, pt: (s, 0, 0)),
        scratch_shapes=[pltpu.VMEM((2, pps, PAGE_SIZE, kv_lora), F32), pltpu.VMEM((2, pps, rope_dim, PAGE_SIZE), F32),
                        pltpu.SemaphoreType.DMA((2, 2)),
                        pltpu.VMEM((DEC_STREAMS, rows, 1), F32), pltpu.VMEM((DEC_STREAMS, rows, 1), F32),
                        pltpu.VMEM((DEC_STREAMS, rows, kv_lora), F32),
                        pltpu.VMEM((DEC_STREAMS, heads, stream_keys), F32),
                        pltpu.VMEM((DEC_STREAMS, rows, stream_keys), F32)])
    return pl.pallas_call(
        functools.partial(_mla_dec_body, layer=layer, heads=heads, t_new=t_new, qk_dim=qk_dim, nope=nope),
        grid_spec=grid_spec,
        out_shape=jax.ShapeDtypeStruct((n, rows, kv_lora), F32),
        compiler_params=_params(2),
        name="mla_dec",
    )(page_table.reshape(-1), qdec, wukt, gkr, cos_t, sin_t, cos_n, sin_n, lat_new, kpe_new_t, cache_lat, cache_kpe_t)


def _log_sigmoid(z):
    return jnp.minimum(z, 0.0) - jnp.log1p(jnp.exp(-jnp.abs(z)))


def _fox_pre_body(x_ref, gmix_ref, wq_ref, wk_ref, wv_ref, wf_ref, bf_ref, gq_ref, gk_ref, tri_ref, pq_ref, pk_ref,
                  cq_ref, ck_ref, vone_ref, qa_out, ka_out, vb_out, kf_out, vf_out, lf_out, carry_sc,
                  *, heads, kv_heads, head_dim, tiles_per_seq):
    i = pl.program_id(0)

    @pl.when(i % tiles_per_seq == 0)
    def _():
        carry_sc[...] = jnp.zeros_like(carry_sc)

    h = _rms(x_ref[...], gmix_ref[...]).astype(BF16)
    tm = h.shape[0]
    lane = lax.broadcasted_iota(jnp.int32, (1, LANES), 1)
    lf = jnp.where(lane < heads, _log_sigmoid(_dot(h, wf_ref[...]) + bf_ref[...]), 0.0)
    lf_out[...] = lf
    cum = carry_sc[...]
    for piece in _split(lf):
        cum = cum + _dot(tri_ref[...], piece)
    carry_sc[...] = cum[tm - 1:tm, :]
    cum_pieces = _split(cum * LOG2E)

    v = _dot(h, wv_ref[...])
    vf_out[...] = v
    vb_out[...] = (v + vone_ref[...]).astype(BF16)
    k = _dot(h, wk_ref[...])
    k_aug = ck_ref[...]
    for j, piece in enumerate(cum_pieces):
        k_aug = k_aug + _dot(piece, pk_ref[j])
    for n in range(kv_heads):
        sl = slice(n * LANES, (n + 1) * LANES)
        kn = _rms(k[:, sl], gk_ref[...], head_dim)
        kf_out[:, sl] = kn
        ka_out[:, sl] = (kn + k_aug[:, sl]).astype(BF16)
    scale = head_dim ** -0.5 * LOG2E
    for p in range(heads // 2):
        cols = slice(p * 2 * LANES, (p + 1) * 2 * LANES)
        q2 = _dot(h, wq_ref[:, cols])
        q_aug = cq_ref[:, cols]
        for j, piece in enumerate(cum_pieces):
            q_aug = q_aug + _dot(piece, pq_ref[j, :, cols])
        for e in range(2):
            sl = slice(e * LANES, (e + 1) * LANES)
            out = slice((2 * p + e) * LANES, (2 * p + e + 1) * LANES)
            qa_out[:, out] = (_rms(q2[:, sl], gq_ref[...], head_dim) * scale + q_aug[:, sl]).astype(BF16)


def _fox_pre(x, gmix, w, seq_rows, heads, kv_heads, head_dim):
    r, d = x.shape
    tm = _tile(math.gcd(seq_rows, r), (512, 256, 128, 64, 32, 16, 8))
    qw, kw = heads * LANES, kv_heads * LANES
    row = lambda c: pl.BlockSpec((tm, c), lambda i: (i, 0))
    tri = (lax.broadcasted_iota(jnp.int32, (tm, tm), 0) >= lax.broadcasted_iota(jnp.int32, (tm, tm), 1)).astype(BF16)
    ins = [x, gmix, w["w_q"], w["w_k"], w["w_v"], w["w_f"], w["b_f"], w["g_q"], w["g_k"], tri,
           w["place_q"], w["place_k"], w["const_q"], w["const_k"], w["v_one"]]
    return pl.pallas_call(
        functools.partial(_fox_pre_body, heads=heads, kv_heads=kv_heads, head_dim=head_dim,
                          tiles_per_seq=seq_rows // tm),
        grid=(r // tm,),
        in_specs=[row(d)] + [_full(a.shape) for a in ins[1:]],
        out_specs=[row(qw), row(kw), row(kw), row(kw), row(kw), row(LANES)],
        out_shape=[jax.ShapeDtypeStruct((r, qw), BF16), jax.ShapeDtypeStruct((r, kw), BF16),
                   jax.ShapeDtypeStruct((r, kw), BF16), jax.ShapeDtypeStruct((r, kw), F32),
                   jax.ShapeDtypeStruct((r, kw), F32), jax.ShapeDtypeStruct((r, LANES), F32)],
        scratch_shapes=[pltpu.VMEM((1, LANES), F32)],
        compiler_params=_params(1),
        name="fox_pre",
    )(*ins)


def _fox_dec_body(pt_ref, q_ref, lfr_ref, kn_ref, vn_ref, wsuf_ref, tblk_ref, ones_blk_ref, k_hbm, v_hbm, lf_hbm,
                  o_ref, k_buf, v_buf, lf_buf, sem, m_sc, l_sc, acc_sc, carry_sc, g_sc, s_sc, fn_sc,
                  *, layer, heads, t_new, n_pages):
    pps = PAGES_PER_STEP
    c = pl.program_id(1)
    nc = pl.num_programs(1)
    rows = heads * t_new

    def base_of(step):
        return lax.div(step, nc) * n_pages + (nc - 1 - lax.rem(step, nc)) * pps

    slot = _page_pipeline(pt_ref, base_of, (k_hbm, v_hbm, lf_hbm), (k_buf, v_buf, lf_buf), sem, layer)

    @pl.when(c == 0)
    def _():
        _softmax_init(m_sc, l_sc, acc_sc)
        carry_sc[...] = jnp.zeros_like(carry_sc)
        fn = jnp.zeros((rows, LANES), F32)
        for piece in _split(lfr_ref[...]):
            fn = fn + _dot(tblk_ref[...], piece)
        fn_sc[...] = fn

    q = q_ref[...]
    fn = fn_sc[...]

    lf = jnp.concatenate([lf_buf[slot, i] for i in range(pps)], axis=0)
    suf = jnp.zeros((pps * heads, 2 * PAGE_SIZE), F32)
    for piece in _split(lf):
        suf = suf + _dot(piece, wsuf_ref[...])
    running = carry_sc[...]
    for i in reversed(range(pps)):
        rsl = slice(i * heads, (i + 1) * heads)
        g_sc[:, i * PAGE_SIZE:(i + 1) * PAGE_SIZE] = (suf[rsl, :PAGE_SIZE] + running) * LOG2E
        running = running + suf[rsl, PAGE_SIZE:]
    carry_sc[...] = running

    per = pps // DEC_STREAMS
    width = per * PAGE_SIZE
    for st in range(DEC_STREAMS):
        pages = range(st * per, (st + 1) * per)
        kt = jnp.concatenate([k_buf[slot, i] for i in pages], axis=1).astype(BF16)
        vt = jnp.concatenate([v_buf[slot, i] for i in pages], axis=1).astype(BF16)
        s = _dot(q, kt)
        for h in range(heads):
            sl = slice(h * t_new, (h + 1) * t_new)
            s_sc[st, sl, :] = s[sl] + g_sc[h:h + 1, st * width:(st + 1) * width] + fn[sl, :1] * LOG2E
        _softmax_update(s_sc[st], vt, m_sc, l_sc, acc_sc, st, v_transposed=True)

    @pl.when(c == pl.num_programs(1) - 1)
    def _():
        rid = lax.broadcasted_iota(jnp.int32, (rows, LANES), 0) % t_new
        cid = lax.broadcasted_iota(jnp.int32, (rows, LANES), 1)
        diag = jnp.where(rid == cid, fn, 0.0)
        fcol = jnp.zeros((rows, LANES), F32)
        for piece in _split(diag):
            fcol = fcol + _dot(ones_blk_ref[...], piece)
        sn = _dot(q, kn_ref[...]) + (fn - fcol) * LOG2E
        sn = jnp.where(cid <= rid, sn, NEG_INF)
        _softmax_update(sn, vn_ref[...], m_sc, l_sc, acc_sc, 0, v_transposed=True)
        o_ref[...] = _softmax_merge(m_sc, l_sc, acc_sc)


def _fox_dec(page_table, q_bd, lf_rows, k_new_t, v_new_t, cache_kt, cache_vt, cache_lft, layer, heads, t_new):
    n, n_pages = page_table.shape
    pps = PAGES_PER_STEP
    assert n_pages % pps == 0 and pps % DEC_STREAMS == 0
    nc = n_pages // pps
    rows = heads * t_new
    width = cache_kt.shape[2]
    chunk_keys = pps * PAGE_SIZE
    assert rows == LANES

    jj = lax.broadcasted_iota(jnp.int32, (PAGE_SIZE, 2 * PAGE_SIZE), 0)
    ss = lax.broadcasted_iota(jnp.int32, (PAGE_SIZE, 2 * PAGE_SIZE), 1)
    wsuf = ((jj > ss) | (ss >= PAGE_SIZE)).astype(BF16)
    ra = lax.broadcasted_iota(jnp.int32, (rows, rows), 0)
    rb = lax.broadcasted_iota(jnp.int32, (rows, rows), 1)
    same_head = (ra // t_new) == (rb // t_new)
    tblk = (same_head & (rb % t_new <= ra % t_new)).astype(BF16)
    ones_blk = same_head.astype(BF16)

    const = lambda shape: pl.BlockSpec(shape, lambda s, c, pt: (0,) * len(shape))
    per_seq = lambda a: pl.BlockSpec((None,) + a.shape[1:], lambda s, c, pt: (s, 0, 0))
    hbm = pl.BlockSpec(memory_space=pl.ANY)
    in_specs = [per_seq(q_bd), per_seq(lf_rows), per_seq(k_new_t), per_seq(v_new_t),
                const(wsuf.shape), const(tblk.shape), const(ones_blk.shape), hbm, hbm, hbm]
    page_buf = lambda arr: pltpu.VMEM((2, pps) + arr.shape[2:], F32)
    grid_spec = pltpu.PrefetchScalarGridSpec(
        num_scalar_prefetch=1, grid=(n, nc), in_specs=in_specs,
        out_specs=pl.BlockSpec((None, rows, width), lambda s, c, pt: (s, 0, 0)),
        scratch_shapes=[page_buf(cache_kt), page_buf(cache_vt), page_buf(cache_lft), pltpu.SemaphoreType.DMA((3, 2)),
                        pltpu.VMEM((DEC_STREAMS, rows, 1), F32), pltpu.VMEM((DEC_STREAMS, rows, 1), F32),
                        pltpu.VMEM((DEC_STREAMS, rows, width), F32),
                        pltpu.VMEM((heads, PAGE_SIZE), F32), pltpu.VMEM((heads, chunk_keys), F32),
                        pltpu.VMEM((DEC_STREAMS, rows, chunk_keys // DEC_STREAMS), F32),
                        pltpu.VMEM((rows, LANES), F32)])
    return pl.pallas_call(
        functools.partial(_fox_dec_body, layer=layer, heads=heads, t_new=t_new, n_pages=n_pages),
        grid_spec=grid_spec,
        out_shape=jax.ShapeDtypeStruct((n, rows, width), F32),
        compiler_params=_params(2),
        name="fox_dec",
    )(page_table.reshape(-1), q_bd, lf_rows, k_new_t, v_new_t, wsuf, tblk, ones_blk, cache_kt, cache_vt, cache_lft)


def _conv_glu_body(x_ref, gmix_ref, w_ref, b_ref, u_out):
    h = _rms(x_ref[...], gmix_ref[...]).astype(BF16)
    ch = u_out.shape[1]
    step = 2 * LANES
    for c in range(ch // step):
        a = _dot(h, w_ref[:, c * step:(c + 1) * step]) + b_ref[:, c * step:(c + 1) * step]
        gate = _dot(h, w_ref[:, ch + c * step:ch + (c + 1) * step]) + b_ref[:, ch + c * step:ch + (c + 1) * step]
        u_out[:, c * step:(c + 1) * step] = a * jax.nn.sigmoid(gate)


def _conv_glu(x, gmix, w, b):
    r, d = x.shape
    ch = w.shape[1] // 2
    tm = _tile(r, (512, 256, 128, 64, 32, 16, 8))
    return pl.pallas_call(
        _conv_glu_body,
        grid=(r // tm,),
        in_specs=[pl.BlockSpec((tm, d), lambda i: (i, 0)), _full((1, d)), _full(w.shape), _full((1, 2 * ch))],
        out_specs=pl.BlockSpec((tm, ch), lambda i: (i, 0)),
        out_shape=jax.ShapeDtypeStruct((r, ch), F32),
        compiler_params=_params(1),
        name="conv_glu",
    )(x, gmix, w.astype(BF16), b.reshape(1, -1))


def _ln_silu(z, g, b):
    mu = jnp.mean(z, axis=-1, keepdims=True)
    zc = z - mu
    y = zc * lax.rsqrt(jnp.mean(zc * zc, axis=-1, keepdims=True) + NORM_EPS) * g + b
    return y * jax.nn.sigmoid(y)


HALO = 32


def _conv_dw_prompt_body(u_ref, halo_ref, w_ref, bdw_ref, g_ref, b_ref, z_out, ext_sc, *, width, tiles_per_seq):
    i = pl.program_id(0)
    tm = u_ref.shape[0]
    first = i % tiles_per_seq == 0
    ext_sc[:HALO, :] = jnp.where(first, 0.0, halo_ref[...])
    ext_sc[HALO:, :] = u_ref[...]
    acc = jnp.zeros(u_ref.shape, F32) + bdw_ref[...]
    off = HALO - (width - 1)
    for w in range(width):
        acc = acc + ext_sc[off + w:off + w + tm, :] * w_ref[w:w + 1, :]
    z_out[...] = _ln_silu(acc, g_ref[...], b_ref[...]).astype(BF16)


def _conv_dw_prompt(u, r, w_dw, b_dw, g_ln, b_ln, seq_rows):
    ch = u.shape[1]
    width = w_dw.shape[0]
    assert width - 1 <= HALO
    tm = _tile(seq_rows, (256, 128, 64, 32))
    ratio = tm // HALO
    vec = lambda a: a.reshape(1, ch)
    return pl.pallas_call(
        functools.partial(_conv_dw_prompt_body, width=width, tiles_per_seq=seq_rows // tm),
        grid=(r // tm,),
        in_specs=[pl.BlockSpec((tm, ch), lambda i: (i, 0)),
                  pl.BlockSpec((HALO, ch), lambda i: (jnp.maximum(i * ratio - 1, 0), 0)),
                  _full((width, ch)), _full((1, ch)), _full((1, ch)), _full((1, ch))],
        out_specs=pl.BlockSpec((tm, ch), lambda i: (i, 0)),
        out_shape=jax.ShapeDtypeStruct((r, ch), BF16),
        scratch_shapes=[pltpu.VMEM((tm + HALO, ch), F32)],
        compiler_params=_params(1),
        name="conv_dw_prompt",
    )(u, u, w_dw, vec(b_dw), vec(g_ln), vec(b_ln))


def _conv_dw_sample_body(ext_ref, w_ref, bdw_ref, g_ref, b_ref, z_out, *, width, t_new):
    acc = jnp.zeros(z_out.shape, F32) + bdw_ref[...]
    for w in range(width):
        acc = acc + ext_ref[:, w:w + t_new, :] * w_ref[w:w + 1, :]
    z_out[...] = _ln_silu(acc, g_ref[...], b_ref[...]).astype(BF16)


def _conv_dw_sample(ext, w_dw, b_dw, g_ln, b_ln, t_new):
    n, rows, ch = ext.shape
    width = w_dw.shape[0]
    nb = _tile(n, (16, 8, 4, 2, 1))
    vec = lambda a: a.reshape(1, ch)
    return pl.pallas_call(
        functools.partial(_conv_dw_sample_body, width=width, t_new=t_new),
        grid=(n // nb,),
        in_specs=[pl.BlockSpec((nb, rows, ch), lambda i: (i, 0, 0)),
                  _full((width, ch)), _full((1, ch)), _full((1, ch)), _full((1, ch))],
        out_specs=pl.BlockSpec((nb, t_new, ch), lambda i: (i, 0, 0)),
        out_shape=jax.ShapeDtypeStruct((n, t_new, ch), BF16),
        compiler_params=_params(1),
        name="conv_dw_sample",
    )(ext, w_dw, vec(b_dw), vec(g_ln), vec(b_ln))


def _pad_last(a, n):
    return jnp.pad(a, [(0, 0)] * (a.ndim - 1) + [(0, n - a.shape[-1])])


def _rope_tables(pos, half):
    inv = ROPE_BASE ** (-jnp.arange(half, dtype=F32) / half)
    ang = pos.astype(F32)[:, None] * inv[None, :]
    return jnp.cos(ang), jnp.sin(ang)


def _mla_layer(x, gmix, pos_rows, dims, cache_lat, cache_kpe, layer, page_table,
               w_dq, g_qa, w_uq, g_qn, w_dkv, g_kva, w_ukv, g_kn, w_o):
    batch, seq, n, t_new = dims
    d = x.shape[1]
    heads, qk_dim = w_uq.shape[1], w_uq.shape[2]
    kv_lora = w_ukv.shape[0]
    nope = w_ukv.shape[2] - w_o.shape[1]
    v_dim = w_o.shape[1]
    rope_dim = qk_dim - nope
    half = rope_dim // 2
    assert nope + rope_dim <= LANES and nope == 64 and rope_dim == 32 and v_dim < LANES
    bs = batch * seq
    past = page_table.shape[1] * PAGE_SIZE

    w_dkv_pad = jnp.concatenate([w_dkv[:, :kv_lora], jnp.zeros((d, nope), F32), w_dkv[:, kv_lora:],
                                 jnp.zeros((d, LANES - qk_dim), F32)], axis=1)
    gslot = lambda g: _pad_last(g.reshape(1, qk_dim), LANES)
    w = dict(
        w_dq=w_dq.astype(BF16), g_qa=g_qa.reshape(1, -1),
        w_uq=_pad_last(w_uq, LANES).reshape(w_uq.shape[0], heads * LANES).astype(BF16), g_q=gslot(g_qn),
        w_dkv=w_dkv_pad.astype(BF16), g_kva=g_kva.reshape(1, -1),
        w_uk=_pad_last(w_ukv[:, :, :nope], LANES).reshape(kv_lora, heads * LANES).astype(BF16),
        w_uv=_pad_last(w_ukv[:, :, nope:], LANES).reshape(kv_lora, heads * LANES).astype(BF16), g_k=gslot(g_kn),
        v_one=_sum_lane_row(heads, v_dim))
    shift = _score_shift(g_qn, g_kn, qk_dim, qk_dim ** -0.5 * LOG2E)
    lane_hot = _sum_lane_row(1, qk_dim)
    w.update(q_one=lane_hot, k_shift=-shift * lane_hot)
    cos, sin = _rope_tables(pos_rows, half)
    r = x.shape[0]
    cos_slot = jnp.concatenate([jnp.ones((r, nope), F32), cos, cos, jnp.ones((r, LANES - qk_dim), F32)], axis=1)
    sin_slot = jnp.concatenate([jnp.zeros((r, nope), F32), -sin, sin, jnp.zeros((r, LANES - qk_dim), F32)], axis=1)
    q, k, v, lat, kpe_slot = _mla_pre(x, gmix.reshape(1, d), w, cos_slot, sin_slot, heads, qk_dim)
    kpe = kpe_slot[:, nope:qk_dim]

    o_p = _flash(q, k, v, shift, batch, seq, heads, 1, 4, v_dim)

    wukt = jnp.transpose(w_ukv[:, :, :nope], (1, 2, 0))
    wcat = jnp.zeros((heads, LANES, kv_lora + LANES), F32)
    wcat = wcat.at[:, :nope, :kv_lora].set(wukt)
    wcat = wcat.at[:, nope:qk_dim, kv_lora:kv_lora + rope_dim].set(jnp.eye(rope_dim, dtype=F32))
    g_dec = _pad_last(jnp.concatenate([g_kn[:nope], jnp.ones((rope_dim,), F32)]).reshape(1, qk_dim), LANES)
    qdec = _mla_qdec(q[bs:], g_dec, wcat.astype(BF16)).reshape(heads, n, t_new, kv_lora + LANES)
    cos_t, sin_t = (a.T for a in _rope_tables(jnp.arange(past), half))
    cos_n, sin_n = (a.T for a in _rope_tables(past + jnp.arange(PAGE_SIZE), half))
    new_rows = lambda a: jnp.pad(a.reshape(n, t_new, -1), ((0, 0), (0, PAGE_SIZE - t_new), (0, 0)))
    o_lat = _mla_dec(page_table, qdec, wukt.reshape(heads * nope, kv_lora).astype(BF16),
                     g_kn[nope:].reshape(rope_dim, 1), cos_t, sin_t, cos_n, sin_n,
                     new_rows(lat[bs:]).astype(BF16), jnp.transpose(new_rows(kpe[bs:]), (0, 2, 1)),
                     cache_lat, jnp.transpose(cache_kpe, (0, 1, 3, 2)), layer, heads, qk_dim, nope)
    w_uv_dec = _pad_last(jnp.transpose(w_ukv[:, :, nope:], (1, 0, 2)), LANES).astype(BF16)
    o_s = _mla_unabsorb(o_lat.reshape(n, heads, t_new, kv_lora), w_uv_dec)

    w_o_pad = jnp.pad(w_o, ((0, 0), (0, LANES - v_dim), (0, 0))).reshape(heads * LANES, d)
    return (o_p, o_s, w_o_pad, jnp.zeros((d,), F32)), lat, kpe


def _fox_layer(x, gmix, dims, cache_k, cache_v, cache_lf, layer, page_table,
               w_q, w_k, w_v, w_f, b_f, g_qn, g_kn, w_o):
    batch, seq, n, t_new = dims
    d = x.shape[1]
    heads, head_dim = w_q.shape[1], w_q.shape[2]
    kv_heads = w_k.shape[1]
    group = heads // kv_heads
    bs = batch * seq
    sp = FORGET_SPLIT
    base = head_dim + 1
    shift_lane = base + sp * (group + 1)
    assert shift_lane < LANES and heads <= LANES
    shift = _score_shift(g_qn, g_kn, head_dim, head_dim ** -0.5 * LOG2E)

    place_q = jnp.zeros((sp, LANES, heads * LANES), F32)
    place_k = jnp.zeros((sp, LANES, kv_heads * LANES), F32)
    const_q = jnp.zeros((1, heads * LANES), F32)
    const_k = jnp.zeros((1, kv_heads * LANES), F32)
    hh = jnp.arange(heads)
    for j in range(sp):
        place_q = place_q.at[j, hh, hh * LANES + base + sp * group + j].set(1.0)
        place_k = place_k.at[j, hh, (hh // group) * LANES + base + sp * (hh % group) + j].set(-1.0)
        const_q = const_q.at[0, hh * LANES + base + sp * (hh % group) + j].set(1.0)
        const_k = const_k.at[0, jnp.arange(kv_heads) * LANES + base + sp * group + j].set(1.0)
    const_q = const_q.at[0, hh * LANES + shift_lane].set(1.0)
    const_k = const_k.at[0, jnp.arange(kv_heads) * LANES + shift_lane].set(-shift)
    gslot = lambda g: _pad_last(g.reshape(1, head_dim), LANES)
    w = dict(w_q=_pad_last(w_q, LANES).reshape(d, heads * LANES).astype(BF16),
             w_k=_pad_last(w_k, LANES).reshape(d, kv_heads * LANES).astype(BF16),
             w_v=_pad_last(w_v, LANES).reshape(d, kv_heads * LANES).astype(BF16),
             w_f=_pad_last(w_f, LANES).astype(BF16), b_f=_pad_last(b_f.reshape(1, heads), LANES),
             g_q=gslot(g_qn), g_k=gslot(g_kn), place_q=place_q.astype(BF16), place_k=place_k.astype(BF16),
             const_q=const_q, const_k=const_k, v_one=_sum_lane_row(kv_heads, head_dim))
    q_aug, k_aug, v_b, k_f, v_f, lf = _fox_pre(x, gmix.reshape(1, d), w, seq, heads, kv_heads, head_dim)
    unslot = lambda a, nh: a.reshape(a.shape[0], nh, LANES)[:, :, :head_dim]
    k_out, v_out, lf_out = unslot(k_f, kv_heads), unslot(v_f, kv_heads), lf[:, :heads]

    o_p = _flash(q_aug, k_aug, v_b, shift, batch, seq, heads, group, group, head_dim)

    kvw = kv_heads * head_dim
    q_s = unslot(q_aug[bs:], heads).reshape(n, t_new, kv_heads, group, head_dim)
    q_bd = jnp.einsum("ntkgd,kj->nkgtjd", q_s.astype(F32), jnp.eye(kv_heads, dtype=F32))
    q_bd = q_bd.reshape(n, heads * t_new, kvw).astype(BF16)
    lf_rows = jnp.broadcast_to(jnp.transpose(lf_out[bs:].reshape(n, t_new, heads), (0, 2, 1))[..., None],
                               (n, heads, t_new, LANES)).reshape(n, heads * t_new, LANES)
    new_t = lambda a: jnp.transpose(jnp.pad(a.reshape(n, t_new, kvw), ((0, 0), (0, PAGE_SIZE - t_new), (0, 0))),
                                    (0, 2, 1)).astype(BF16)
    pool = cache_k.shape[1]
    page_t = lambda c: jnp.transpose(c, (0, 1, 3, 4, 2)).reshape(c.shape[0], pool, kvw, PAGE_SIZE)
    o_bd = _fox_dec(page_table, q_bd, lf_rows, new_t(k_out[bs:]), new_t(v_out[bs:]), page_t(cache_k), page_t(cache_v),
                    jnp.transpose(cache_lf, (0, 1, 3, 2)), layer, heads, t_new)
    o_s = jnp.einsum("nkgtjd,kj->ntkgd", o_bd.reshape(n, kv_heads, group, t_new, kv_heads, head_dim),
                     jnp.eye(kv_heads, dtype=F32))
    o_s = _pad_last(o_s.reshape(n * t_new, heads, head_dim), LANES).reshape(n * t_new, heads * LANES).astype(BF16)

    w_o_pad = jnp.pad(w_o, ((0, 0), (0, LANES - head_dim), (0, 0))).reshape(heads * LANES, d)
    return (o_p, o_s, w_o_pad, jnp.zeros((d,), F32)), k_out, v_out, lf_out


def _conv_layer(x, gmix, dims, state, w_pw1, b_pw1, w_dw, b_dw, g_ln, b_ln, w_pw2, b_pw2):
    batch, seq, n, t_new = dims
    d = x.shape[1]
    bs = batch * seq
    keep = w_dw.shape[0] - 1
    assert seq >= keep
    u = _conv_glu(x, gmix.reshape(1, d), w_pw1, b_pw1)
    ch = u.shape[1]
    z_p = _conv_dw_prompt(u, bs, w_dw, b_dw, g_ln, b_ln, seq)
    ext_s = jnp.concatenate([state.astype(F32), u[bs:].reshape(n, t_new, ch)], axis=1)
    rows = -(-(keep + t_new) // 8) * 8
    z_s = _conv_dw_sample(jnp.pad(ext_s, ((0, 0), (0, rows - keep - t_new), (0, 0))), w_dw, b_dw, g_ln, b_ln, t_new)
    return (z_p, z_s.reshape(n * t_new, ch), w_pw2, b_pw2), u[:bs].reshape(batch, seq, ch)[:, seq - keep:], ext_s[:, t_new:]


def kernel(x_prompt, x_sample, cache_mla_latent, cache_mla_krope, cache_fox_k, cache_fox_v, cache_fox_logf, state_conv, page_table, norm_mix, norm_mlp, mlp_up, mlp_down, mla_w_dq, mla_g_qa, mla_w_uq, mla_g_qn, mla_w_dkv, mla_g_kva, mla_w_ukv, mla_g_kn, mla_w_o, fox_w_q, fox_w_k, fox_w_v, fox_w_f, fox_b_f, fox_g_qn, fox_g_kn, fox_w_o, conv_w_pw1, conv_b_pw1, conv_w_dw, conv_b_dw, conv_g_ln, conv_b_ln, conv_w_pw2, conv_b_pw2):
    batch, seq, d = x_prompt.shape
    n, t_new, _ = x_sample.shape
    dims = (batch, seq, n, t_new)
    bs = batch * seq
    past = page_table.shape[1] * PAGE_SIZE
    depth = norm_mix.shape[0]
    x = jnp.concatenate([x_prompt.reshape(bs, d), x_sample.reshape(n * t_new, d)], axis=0)
    pos_rows = jnp.concatenate([jnp.tile(jnp.arange(seq), batch), jnp.tile(past + jnp.arange(t_new), n)])

    outs = {name: [] for name in ("lat", "kpe", "fk", "fv", "flf", "cp", "cs")}
    for i in range(depth):
        kind, j = i % 3, i // 3
        if kind == 0:
            mix, lat, kpe = _mla_layer(x, norm_mix[i], pos_rows, dims, cache_mla_latent, cache_mla_krope, j, page_table,
                                       mla_w_dq[j], mla_g_qa[j], mla_w_uq[j], mla_g_qn[j], mla_w_dkv[j], mla_g_kva[j],
                                       mla_w_ukv[j], mla_g_kn[j], mla_w_o[j])
            outs["lat"].append(lat)
            outs["kpe"].append(kpe)
        elif kind == 1:
            mix, fk, fv, flf = _fox_layer(x, norm_mix[i], dims, cache_fox_k, cache_fox_v, cache_fox_logf, j, page_table,
                                          fox_w_q[j], fox_w_k[j], fox_w_v[j], fox_w_f[j], fox_b_f[j], fox_g_qn[j],
                                          fox_g_kn[j], fox_w_o[j])
            outs["fk"].append(fk)
            outs["fv"].append(fv)
            outs["flf"].append(flf)
        else:
            mix, cp, cs = _conv_layer(x, norm_mix[i], dims, state_conv[j], conv_w_pw1[j], conv_b_pw1[j], conv_w_dw[j],
                                      conv_b_dw[j], conv_g_ln[j], conv_b_ln[j], conv_w_pw2[j], conv_b_pw2[j])
            outs["cp"].append(cp)
            outs["cs"].append(cs)
        x = _mlp(x, *mix, norm_mlp[i], mlp_up[i], mlp_down[i])

    def split_rows(name):
        stacked = jnp.stack(outs[name])
        tail = stacked.shape[2:]
        return (stacked[:, :bs].reshape((-1, batch, seq) + tail), stacked[:, bs:].reshape((-1, n, t_new) + tail))

    lat_p, lat_s = split_rows("lat")
    kpe_p, kpe_s = split_rows("kpe")
    fk_p, fk_s = split_rows("fk")
    fv_p, fv_s = split_rows("fv")
    flf_p, flf_s = split_rows("flf")
    return (x[:bs].reshape(batch, seq, d), x[bs:].reshape(n, t_new, d), lat_p, lat_s, kpe_p, kpe_s,
            fk_p, fk_s, fv_p, fv_s, flf_p, flf_s, jnp.stack(outs["cp"]), jnp.stack(outs["cs"]))
```

```python
import functools
import math

import jax
import jax.numpy as jnp
from jax import lax
from jax.experimental import pallas as pl
from jax.experimental.pallas import tpu as pltpu

F32 = jnp.float32
BF16 = jnp.bfloat16

NORM_EPS = 1e-6
NEG_INF = -1e30
ROPE_BASE = 10000.0
LOG2E = math.log2(math.e)
FORGET_SPLIT = 3
LANES = 128
PAGE_SIZE = 128
PAGES_PER_STEP = 16
DEC_STREAMS = 2
VMEM_LIMIT = 56 * 1024 * 1024

NT_DIMS = (((1,), (1,)), ((), ()))


def _params(n_axes):
    return pltpu.CompilerParams(dimension_semantics=("arbitrary",) * n_axes, vmem_limit_bytes=VMEM_LIMIT)


def _tile(n, cands):
    for c in cands:
        if n % c == 0:
            return c
    raise ValueError(f"no tile for {n}")


def _dot(a, b):
    return jnp.dot(a, b, preferred_element_type=F32)


def _dot_nt(a, b):
    return lax.dot_general(a, b, NT_DIMS, preferred_element_type=F32)


def _rms(x, g, n=None):
    n = x.shape[-1] if n is None else n
    ss = jnp.sum(x * x, axis=-1, keepdims=True)
    return x * lax.rsqrt(ss * (1.0 / n) + NORM_EPS) * g


def _split(x, pieces=FORGET_SPLIT):
    out = []
    for _ in range(pieces):
        p = x.astype(BF16)
        out.append(p)
        x = x - p.astype(F32)
    return out


def _full(shape):
    nd = len(shape)
    return pl.BlockSpec(shape, lambda *_: (0,) * nd)


def _mlp_body(x_ref, ap_ref, as_ref, wa_ref, ba_ref, g_ref, wu_ref, wd_ref, o_ref, x1_sc, h_sc, acc_sc, *, prompt_tiles):
    i, f = pl.program_id(0), pl.program_id(1)

    def first(a_ref):
        x1 = x_ref[...] + _dot(a_ref[...], wa_ref[...]) + ba_ref[...]
        x1_sc[...] = x1
        h_sc[...] = _rms(x1, g_ref[...]).astype(BF16)
        acc_sc[...] = jnp.zeros_like(acc_sc)

    pl.when((f == 0) & (i < prompt_tiles))(lambda: first(ap_ref))
    pl.when((f == 0) & (i >= prompt_tiles))(lambda: first(as_ref))

    a = jnp.maximum(_dot(h_sc[...], wu_ref[...]), 0.0)
    acc_sc[...] += _dot((a * a).astype(BF16), wd_ref[...])

    @pl.when(f == pl.num_programs(1) - 1)
    def _():
        o_ref[...] = x1_sc[...] + acc_sc[...]


def _mlp(x, a_p, a_s, w_a, b_a, g, w_up, w_down):
    r, d = x.shape
    k = a_p.shape[1]
    dff = w_up.shape[1]
    bs = r - a_s.shape[0]
    tm = _tile(math.gcd(bs, a_s.shape[0]), (512, 256, 128, 64, 32, 16, 8))
    tf = _tile(dff, (1024, 512, 256, 128))
    pt = bs // tm
    return pl.pallas_call(
        functools.partial(_mlp_body, prompt_tiles=pt),
        grid=(r // tm, dff // tf),
        in_specs=[pl.BlockSpec((tm, d), lambda i, f: (i, 0)),
                  pl.BlockSpec((tm, k), lambda i, f: (jnp.minimum(i, pt - 1), 0)),
                  pl.BlockSpec((tm, k), lambda i, f: (jnp.maximum(i - pt, 0), 0)),
                  pl.BlockSpec((k, d), lambda i, f: (0, 0)),
                  pl.BlockSpec((1, d), lambda i, f: (0, 0)),
                  pl.BlockSpec((1, d), lambda i, f: (0, 0)),
                  pl.BlockSpec((d, tf), lambda i, f: (0, f)),
                  pl.BlockSpec((tf, d), lambda i, f: (f, 0))],
        out_specs=pl.BlockSpec((tm, d), lambda i, f: (i, 0)),
        out_shape=jax.ShapeDtypeStruct((r, d), F32),
        scratch_shapes=[pltpu.VMEM((tm, d), F32), pltpu.VMEM((tm, d), BF16), pltpu.VMEM((tm, d), F32)],
        compiler_params=_params(2),
        name="mlp",
    )(x, a_p, a_s, w_a.astype(BF16), b_a.reshape(1, d).astype(F32), g.reshape(1, d), w_up.astype(BF16),
      w_down.astype(BF16))


def _flash_body(q_ref, k_ref, v_ref, o_ref, m_sc, acc_sc, *, t, kv_of, sum_lane):
    qi = pl.program_id(2)
    streams = len(kv_of)
    m_sc[...] = jnp.full_like(m_sc, NEG_INF)
    acc_sc[...] = jnp.zeros_like(acc_sc)

    def step(j, masked):
        start = pl.multiple_of(j * t, t)
        for g in range(streams):
            kv = slice(kv_of[g] * LANES, (kv_of[g] + 1) * LANES)
            s = _dot_nt(q_ref[:, g * LANES:(g + 1) * LANES], k_ref[pl.ds(start, t), kv])
            if masked:
                row = lax.broadcasted_iota(jnp.int32, (t, t), 0)
                col = lax.broadcasted_iota(jnp.int32, (t, t), 1)
                s = jnp.where(col <= row, s, NEG_INF)
            m_prev = m_sc[g]
            m_new = jnp.maximum(m_prev, jnp.max(s, axis=-1, keepdims=True))
            p = jnp.exp2(s - m_new).astype(BF16)
            acc_sc[g] = jnp.exp2(m_prev - m_new) * acc_sc[g] + _dot(p, v_ref[pl.ds(start, t), kv])
            m_sc[g] = m_new

    def unmasked(j, carry):
        step(j, False)
        return carry

    lax.fori_loop(0, qi, unmasked, 0)
    step(qi, True)
    for g in range(streams):
        acc = acc_sc[g]
        o_ref[:, g * LANES:(g + 1) * LANES] = (acc / acc[:, sum_lane:sum_lane + 1]).astype(o_ref.dtype)


def _flash_shifted_body(q_ref, k_ref, v_ref, o_ref, acc_sc, *, t, kv_of, sum_lane):
    qi = pl.program_id(2)
    streams = len(kv_of)
    acc_sc[...] = jnp.zeros_like(acc_sc)

    def step(j, masked):
        start = pl.multiple_of(j * t, t)
        for g in range(streams):
            kv = slice(kv_of[g] * LANES, (kv_of[g] + 1) * LANES)
            s = _dot_nt(q_ref[:, g * LANES:(g + 1) * LANES], k_ref[pl.ds(start, t), kv])
            if masked:
                row = lax.broadcasted_iota(jnp.int32, (t, t), 0)
                col = lax.broadcasted_iota(jnp.int32, (t, t), 1)
                s = jnp.where(col <= row, s, NEG_INF)
            acc_sc[g] += _dot(jnp.exp2(s).astype(BF16), v_ref[pl.ds(start, t), kv])

    def unmasked(j, carry):
        step(j, False)
        return carry

    lax.fori_loop(0, qi, unmasked, 0)
    step(qi, True)
    for g in range(streams):
        acc = acc_sc[g]
        o_ref[:, g * LANES:(g + 1) * LANES] = (acc / acc[:, sum_lane:sum_lane + 1]).astype(o_ref.dtype)


MAX_SHIFT = 60.0


def _score_shift(g_q, g_k, n, scale):
    bound = n * jnp.max(jnp.abs(g_q)) * jnp.max(jnp.abs(g_k)) * scale
    return jnp.ceil(bound * 1.02 + 0.5)


def _flash(q, k, v, shift, batch, seq, heads, group, streams, sum_lane):
    t = _tile(seq, (512, 256, 128, 64, 32, 16))
    nq = seq // t
    assert heads % streams == 0 and (streams % group == 0 or group % streams == 0)
    if streams >= group:
        kv_w, kv_of = streams // group, tuple(g // group for g in range(streams))
        kv_idx = lambda hb: hb
    else:
        kv_w, kv_of = 1, (0,) * streams
        kv_idx = lambda hb: hb * streams // group

    def call(body, scratch, name):
        return pl.pallas_call(
            functools.partial(body, t=t, kv_of=kv_of, sum_lane=sum_lane),
            grid=(batch, heads // streams, nq),
            in_specs=[pl.BlockSpec((t, streams * LANES), lambda b, h, i: (b * nq + i, h)),
                      pl.BlockSpec((seq, kv_w * LANES), lambda b, h, i: (b, kv_idx(h))),
                      pl.BlockSpec((seq, kv_w * LANES), lambda b, h, i: (b, kv_idx(h)))],
            out_specs=pl.BlockSpec((t, streams * LANES), lambda b, h, i: (b * nq + i, h)),
            out_shape=jax.ShapeDtypeStruct((batch * seq, heads * LANES), BF16),
            scratch_shapes=scratch,
            compiler_params=_params(3),
            name=name,
        )

    acc = pltpu.VMEM((streams, t, LANES), F32)
    shifted = call(_flash_shifted_body, [acc], "flash_shifted")
    online = call(_flash_body, [pltpu.VMEM((streams, t, 1), F32), acc], "flash")
    return lax.cond(shift <= MAX_SHIFT, shifted, online, q, k, v)


def _sum_lane_row(heads, lane):
    return jnp.zeros((heads, LANES), F32).at[:, lane].set(1.0).reshape(1, heads * LANES)


def _swap_rope_halves(a, nope, half):
    out = jnp.zeros_like(a)
    out = out.at[..., nope:nope + half].set(a[..., nope + half:nope + 2 * half])
    return out.at[..., nope + half:nope + 2 * half].set(a[..., nope:nope + half])


def _mla_pre_body(x_ref, gmix_ref, wdq_ref, gqa_ref, wuq_ref, wuqs_ref, gq_ref, gqs_ref, wdkv_ref, gkva_ref, wuk_ref,
                  wuv_ref, gk_ref, gks_ref, vone_ref, qone_ref, kshift_ref, cos_ref, sin_ref,
                  q_out, k_out, v_out, lat_out, kpe_out, *, heads, qk_dim):
    h = _rms(x_ref[...], gmix_ref[...]).astype(BF16)
    cq = _rms(_dot(h, wdq_ref[...]), gqa_ref[...]).astype(BF16)
    a = _dot(h, wdkv_ref[...])
    kv_lora = gkva_ref.shape[1]
    lat = _rms(a[:, :kv_lora], gkva_ref[...])
    lat_out[...] = lat
    kpe = a[:, kv_lora:kv_lora + LANES]
    kpe_sw = a[:, kv_lora + LANES:]
    kpe_out[...] = kpe
    latb = lat.astype(BF16)
    cos, sin = cos_ref[...], sin_ref[...]
    q_cos, q_sin = gq_ref[...] * cos, gqs_ref[...] * sin
    k_cos, k_sin_sw = gk_ref[...] * cos, kpe_sw * (gks_ref[...] * sin)
    scale = qk_dim ** -0.5 * LOG2E

    def norm_factor(x):
        return lax.rsqrt(jnp.sum(x * x, axis=-1, keepdims=True) * (1.0 / qk_dim) + NORM_EPS)

    for p in range(heads // 2):
        cols = slice(p * 2 * LANES, (p + 1) * 2 * LANES)
        q2 = _dot(cq, wuq_ref[:, cols])
        q2s = _dot(cq, wuqs_ref[:, cols])
        k2 = _dot(latb, wuk_ref[:, cols])
        v_out[:, cols] = (_dot(latb, wuv_ref[:, cols]) + vone_ref[:, cols]).astype(BF16)
        for e in range(2):
            sl = slice(e * LANES, (e + 1) * LANES)
            out = slice((2 * p + e) * LANES, (2 * p + e + 1) * LANES)
            qh = q2[:, sl]
            q_rot = (qh * q_cos + q2s[:, sl] * q_sin) * (norm_factor(qh) * scale)
            q_out[:, out] = (q_rot + qone_ref[...]).astype(BF16)
            kh = k2[:, sl] + kpe
            k_rot = (kh * k_cos + k_sin_sw) * norm_factor(kh)
            k_out[:, out] = (k_rot + kshift_ref[...]).astype(BF16)


def _mla_pre(x, gmix, w, cos, sin, heads, qk_dim):
    r, d = x.shape
    tm = _tile(r, (512, 256, 128, 64, 32, 16, 8))
    hw = heads * LANES
    row = lambda c: pl.BlockSpec((tm, c), lambda i: (i, 0))
    ins = [x, gmix, w["w_dq"], w["g_qa"], w["w_uq"], w["w_uq_sw"], w["g_q"], w["g_q_sw"], w["w_dkv"], w["g_kva"],
           w["w_uk"], w["w_uv"], w["g_k"], w["g_k_sw"], w["v_one"], w["q_one"], w["k_shift"]]
    return pl.pallas_call(
        functools.partial(_mla_pre_body, heads=heads, qk_dim=qk_dim),
        grid=(r // tm,),
        in_specs=[row(d)] + [_full(a.shape) for a in ins[1:]] + [row(LANES), row(LANES)],
        out_specs=[row(hw), row(hw), row(hw), row(w["g_kva"].shape[1]), row(LANES)],
        out_shape=[jax.ShapeDtypeStruct((r, hw), BF16)] * 3
        + [jax.ShapeDtypeStruct((r, w["g_kva"].shape[1]), F32), jax.ShapeDtypeStruct((r, LANES), F32)],
        compiler_params=_params(1),
        name="mla_pre",
    )(*ins, cos, sin)


def _mla_qdec_body(q_ref, gk_ref, w_ref, o_ref):
    o_ref[...] = _dot((q_ref[...].astype(F32) * gk_ref[...]).astype(BF16), w_ref[...])


def _mla_qdec(q_s, gk, wcat):
    nt = q_s.shape[0]
    heads, _, wd = wcat.shape
    return pl.pallas_call(
        _mla_qdec_body,
        grid=(heads,),
        in_specs=[pl.BlockSpec((nt, LANES), lambda h: (0, h)), _full((1, LANES)),
                  pl.BlockSpec((None, LANES, wd), lambda h: (h, 0, 0))],
        out_specs=pl.BlockSpec((None, nt, wd), lambda h: (h, 0, 0)),
        out_shape=jax.ShapeDtypeStruct((heads, nt, wd), F32),
        compiler_params=_params(1),
        name="mla_qdec",
    )(q_s, gk, wcat)


def _mla_unabsorb_body(o_ref, w_ref, out_ref):
    n, t, c = o_ref.shape
    out_ref[...] = _dot(o_ref[...].reshape(n * t, c).astype(BF16), w_ref[...]).astype(BF16)


def _mla_unabsorb(o_lat, w_uv):
    n, heads, t, c = o_lat.shape
    return pl.pallas_call(
        _mla_unabsorb_body,
        grid=(heads,),
        in_specs=[pl.BlockSpec((n, None, t, c), lambda h: (0, h, 0, 0)),
                  pl.BlockSpec((None, c, LANES), lambda h: (h, 0, 0))],
        out_specs=pl.BlockSpec((n * t, LANES), lambda h: (0, h)),
        out_shape=jax.ShapeDtypeStruct((n * t, heads * LANES), BF16),
        compiler_params=_params(1),
        name="mla_unabsorb",
    )(o_lat, w_uv)


def _softmax_update(s, v, m_sc, l_sc, acc_sc, st, v_transposed=False):
    m_prev = m_sc[st]
    m_new = jnp.maximum(m_prev, jnp.max(s, axis=-1, keepdims=True))
    alpha = jnp.exp2(m_prev - m_new)
    p = jnp.exp2(s - m_new)
    l_sc[st] = alpha * l_sc[st] + jnp.sum(p, axis=-1, keepdims=True)
    pv = _dot_nt(p.astype(BF16), v) if v_transposed else _dot(p.astype(BF16), v)
    acc_sc[st] = alpha * acc_sc[st] + pv
    m_sc[st] = m_new


def _softmax_init(m_sc, l_sc, acc_sc):
    m_sc[...] = jnp.full_like(m_sc, NEG_INF)
    l_sc[...] = jnp.zeros_like(l_sc)
    acc_sc[...] = jnp.zeros_like(acc_sc)


def _softmax_merge(m_sc, l_sc, acc_sc):
    m = m_sc[0]
    for st in range(1, DEC_STREAMS):
        m = jnp.maximum(m, m_sc[st])
    l = jnp.zeros_like(m)
    acc = jnp.zeros(acc_sc.shape[1:], F32)
    for st in range(DEC_STREAMS):
        w = jnp.exp2(m_sc[st] - m)
        l = l + w * l_sc[st]
        acc = acc + w * acc_sc[st]
    return acc / l


def _page_pipeline(pt_ref, base_of, hbm_refs, bufs, sem, layer):
    n_steps = pl.num_programs(0) * pl.num_programs(1)
    g = pl.program_id(0) * pl.num_programs(1) + pl.program_id(1)
    slot = lax.rem(g, 2)
    pps = bufs[0].shape[1]

    def copies(step, sl):
        out = []
        for i in range(pps):
            pid = 0 if step is None else pt_ref[base_of(step) + i]
            for a, (hbm, buf) in enumerate(zip(hbm_refs, bufs)):
                out.append(pltpu.make_async_copy(hbm.at[layer, pid], buf.at[sl, i], sem.at[a, sl]))
        return out

    @pl.when(g == 0)
    def _():
        for cp in copies(g, slot):
            cp.start()

    @pl.when(g + 1 < n_steps)
    def _():
        for cp in copies(g + 1, 1 - slot):
            cp.start()

    for cp in copies(None, slot):
        cp.wait()
    return slot


def _mla_dec_body(pt_ref, q_ref, wukt_ref, gkr_ref, cos_ref, sin_ref, cosn_ref, sinn_ref, latn_ref, kpen_ref,
                  lat_hbm, kpe_hbm, o_ref, lat_buf, kpe_buf, sem, m_sc, l_sc, acc_sc, rinv_sc, s_sc,
                  *, layer, heads, t_new, qk_dim, nope):
    pps = PAGES_PER_STEP
    c = pl.program_id(1)
    kv_lora = wukt_ref.shape[1]
    rows = heads * t_new
    slot = _page_pipeline(pt_ref, lambda step: step * pps, (lat_hbm, kpe_hbm), (lat_buf, kpe_buf), sem, layer)

    @pl.when(c == 0)
    def _():
        _softmax_init(m_sc, l_sc, acc_sc)

    q = q_ref[...].reshape(rows, q_ref.shape[-1]).astype(BF16)
    q_lat, q_rope = q[:, :kv_lora], q[:, kv_lora:kv_lora + LANES]
    rope_dim = qk_dim - nope
    half = rope_dim // 2

    def chunk(lat, kpt, cos, sin, st, new):
        n_keys = lat.shape[0]
        kt = _dot_nt(wukt_ref[...], lat)
        ss = jnp.sum((kt * kt).reshape(heads, nope, n_keys), axis=1)
        ss = ss + jnp.sum(kpt * kpt, axis=0, keepdims=True)
        rinv_sc[st, :, :n_keys] = lax.rsqrt(ss * (1.0 / qk_dim) + NORM_EPS)
        kg = kpt * gkr_ref[...]
        x1, x2 = kg[:half], kg[half:]
        krot = jnp.concatenate([x1 * cos - x2 * sin, x2 * cos + x1 * sin,
                                jnp.zeros((LANES - rope_dim, n_keys), F32)], axis=0).astype(BF16)
        s = _dot_nt(q_lat, lat) + _dot(q_rope, krot)
        for h in range(heads):
            sl = slice(h * t_new, (h + 1) * t_new)
            s_sc[st, sl, :n_keys] = s[sl] * rinv_sc[st, h:h + 1, :n_keys]
        s = s_sc[st, :, :n_keys]
        if new:
            tq = lax.broadcasted_iota(jnp.int32, (rows, n_keys), 0) % t_new
            tk = lax.broadcasted_iota(jnp.int32, (rows, n_keys), 1)
            s = jnp.where(tk <= tq, s, NEG_INF)
        _softmax_update(s, lat, m_sc, l_sc, acc_sc, st)

    per = pps // DEC_STREAMS
    for st in range(DEC_STREAMS):
        pages = range(st * per, (st + 1) * per)
        lat = jnp.concatenate([lat_buf[slot, i] for i in pages], axis=0).astype(BF16)
        kpt = jnp.concatenate([kpe_buf[slot, i] for i in pages], axis=1)
        tab = c * DEC_STREAMS + st
        chunk(lat, kpt, cos_ref[tab], sin_ref[tab], st, False)

    @pl.when(c == pl.num_programs(1) - 1)
    def _():
        chunk(latn_ref[...], kpen_ref[...], cosn_ref[...], sinn_ref[...], 0, True)
        o_ref[...] = _softmax_merge(m_sc, l_sc, acc_sc)


def _mla_dec(page_table, qdec, wukt, gkr, cos_t, sin_t, cos_n, sin_n, lat_new, kpe_new_t, cache_lat, cache_kpe_t,
             layer, heads, qk_dim, nope):
    n, n_pages = page_table.shape
    t_new = qdec.shape[2]
    kv_lora = wukt.shape[1]
    rope_dim = qk_dim - nope
    pps = PAGES_PER_STEP
    assert n_pages % pps == 0 and pps % DEC_STREAMS == 0 and t_new <= PAGE_SIZE
    nc = n_pages // pps
    rows = heads * t_new
    stream_keys = pps * PAGE_SIZE // DEC_STREAMS
    by_stream = lambda tab: jnp.transpose(tab.reshape(tab.shape[0], -1, stream_keys), (1, 0, 2))
    cos_t, sin_t = by_stream(cos_t), by_stream(sin_t)

    const = lambda shape: pl.BlockSpec(shape, lambda s, c, pt: (0,) * len(shape))
    hbm = pl.BlockSpec(memory_space=pl.ANY)
    in_specs = [pl.BlockSpec((heads, None, t_new, qdec.shape[3]), lambda s, c, pt: (0, s, 0, 0)),
                const(wukt.shape), const(gkr.shape), const(cos_t.shape), const(sin_t.shape),
                const(cos_n.shape), const(sin_n.shape),
                pl.BlockSpec((None, PAGE_SIZE, kv_lora), lambda s, c, pt: (s, 0, 0)),
                pl.BlockSpec((None, rope_dim, PAGE_SIZE), lambda s, c, pt: (s, 0, 0)),
                hbm, hbm]
    grid_spec = pltpu.PrefetchScalarGridSpec(
        num_scalar_prefetch=1, grid=(n, nc), in_specs=in_specs,
        out_specs=pl.BlockSpec((None, rows, kv_lora), lambda s, c, pt: (s, 0, 0)),
        scratch_shapes=[pltpu.VMEM((2, pps, PAGE_SIZE, kv_lora), F32), pltpu.VMEM((2, pps, rope_dim, PAGE_SIZE), F32),
                        pltpu.SemaphoreType.DMA((2, 2)),
                        pltpu.VMEM((DEC_STREAMS, rows, 1), F32), pltpu.VMEM((DEC_STREAMS, rows, 1), F32),
                        pltpu.VMEM((DEC_STREAMS, rows, kv_lora), F32),
                        pltpu.VMEM((DEC_STREAMS, heads, stream_keys), F32),
                        pltpu.VMEM((DEC_STREAMS, rows, stream_keys), F32)])
    return pl.pallas_call(
        functools.partial(_mla_dec_body, layer=layer, heads=heads, t_new=t_new, qk_dim=qk_dim, nope=nope),
        grid_spec=grid_spec,
        out_shape=jax.ShapeDtypeStruct((n, rows, kv_lora), F32),
        compiler_params=_params(2),
        name="mla_dec",
    )(page_table.reshape(-1), qdec, wukt, gkr, cos_t, sin_t, cos_n, sin_n, lat_new, kpe_new_t, cache_lat, cache_kpe_t)


def _log_sigmoid(z):
    return jnp.minimum(z, 0.0) - jnp.log1p(jnp.exp(-jnp.abs(z)))


def _fox_pre_body(x_ref, gmix_ref, wq_ref, wk_ref, wv_ref, wf_ref, bf_ref, gq_ref, gk_ref, tri_ref, pq_ref, pk_ref,
                  cq_ref, ck_ref, vone_ref, qa_out, ka_out, vb_out, kf_out, vf_out, lf_out, carry_sc,
                  *, heads, kv_heads, head_dim, tiles_per_seq):
    i = pl.program_id(0)

    @pl.when(i % tiles_per_seq == 0)
    def _():
        carry_sc[...] = jnp.zeros_like(carry_sc)

    h = _rms(x_ref[...], gmix_ref[...]).astype(BF16)
    tm = h.shape[0]
    lane = lax.broadcasted_iota(jnp.int32, (1, LANES), 1)
    lf = jnp.where(lane < heads, _log_sigmoid(_dot(h, wf_ref[...]) + bf_ref[...]), 0.0)
    lf_out[...] = lf
    cum = carry_sc[...]
    for piece in _split(lf):
        cum = cum + _dot(tri_ref[...], piece)
    carry_sc[...] = cum[tm - 1:tm, :]
    cum_pieces = _split(cum * LOG2E)

    v = _dot(h, wv_ref[...])
    vf_out[...] = v
    vb_out[...] = (v + vone_ref[...]).astype(BF16)
    k = _dot(h, wk_ref[...])
    k_aug = ck_ref[...]
    for j, piece in enumerate(cum_pieces):
        k_aug = k_aug + _dot(piece, pk_ref[j])
    for n in range(kv_heads):
        sl = slice(n * LANES, (n + 1) * LANES)
        kn = _rms(k[:, sl], gk_ref[...], head_dim)
        kf_out[:, sl] = kn
        ka_out[:, sl] = (kn + k_aug[:, sl]).astype(BF16)
    scale = head_dim ** -0.5 * LOG2E
    for p in range(heads // 2):
        cols = slice(p * 2 * LANES, (p + 1) * 2 * LANES)
        q2 = _dot(h, wq_ref[:, cols])
        q_aug = cq_ref[:, cols]
        for j, piece in enumerate(cum_pieces):
            q_aug = q_aug + _dot(piece, pq_ref[j, :, cols])
        for e in range(2):
            sl = slice(e * LANES, (e + 1) * LANES)
            out = slice((2 * p + e) * LANES, (2 * p + e + 1) * LANES)
            qa_out[:, out] = (_rms(q2[:, sl], gq_ref[...], head_dim) * scale + q_aug[:, sl]).astype(BF16)


def _fox_pre(x, gmix, w, seq_rows, heads, kv_heads, head_dim):
    r, d = x.shape
    tm = _tile(math.gcd(seq_rows, r), (512, 256, 128, 64, 32, 16, 8))
    qw, kw = heads * LANES, kv_heads * LANES
    row = lambda c: pl.BlockSpec((tm, c), lambda i: (i, 0))
    tri = (lax.broadcasted_iota(jnp.int32, (tm, tm), 0) >= lax.broadcasted_iota(jnp.int32, (tm, tm), 1)).astype(BF16)
    ins = [x, gmix, w["w_q"], w["w_k"], w["w_v"], w["w_f"], w["b_f"], w["g_q"], w["g_k"], tri,
           w["place_q"], w["place_k"], w["const_q"], w["const_k"], w["v_one"]]
    return pl.pallas_call(
        functools.partial(_fox_pre_body, heads=heads, kv_heads=kv_heads, head_dim=head_dim,
                          tiles_per_seq=seq_rows // tm),
        grid=(r // tm,),
        in_specs=[row(d)] + [_full(a.shape) for a in ins[1:]],
        out_specs=[row(qw), row(kw), row(kw), row(kw), row(kw), row(LANES)],
        out_shape=[jax.ShapeDtypeStruct((r, qw), BF16), jax.ShapeDtypeStruct((r, kw), BF16),
                   jax.ShapeDtypeStruct((r, kw), BF16), jax.ShapeDtypeStruct((r, kw), F32),
                   jax.ShapeDtypeStruct((r, kw), F32), jax.ShapeDtypeStruct((r, LANES), F32)],
        scratch_shapes=[pltpu.VMEM((1, LANES), F32)],
        compiler_params=_params(1),
        name="fox_pre",
    )(*ins)


def _fox_dec_body(pt_ref, q_ref, lfr_ref, kn_ref, vn_ref, wsuf_ref, tblk_ref, ones_blk_ref, k_hbm, v_hbm, lf_hbm,
                  o_ref, k_buf, v_buf, lf_buf, sem, m_sc, l_sc, acc_sc, carry_sc, g_sc, s_sc, fn_sc,
                  *, layer, heads, t_new, n_pages):
    pps = PAGES_PER_STEP
    c = pl.program_id(1)
    nc = pl.num_programs(1)
    rows = heads * t_new

    def base_of(step):
        return lax.div(step, nc) * n_pages + (nc - 1 - lax.rem(step, nc)) * pps

    slot = _page_pipeline(pt_ref, base_of, (k_hbm, v_hbm, lf_hbm), (k_buf, v_buf, lf_buf), sem, layer)

    @pl.when(c == 0)
    def _():
        _softmax_init(m_sc, l_sc, acc_sc)
        carry_sc[...] = jnp.zeros_like(carry_sc)
        fn = jnp.zeros((rows, LANES), F32)
        for piece in _split(lfr_ref[...]):
            fn = fn + _dot(tblk_ref[...], piece)
        fn_sc[...] = fn

    q = q_ref[...]
    fn = fn_sc[...]

    lf = jnp.concatenate([lf_buf[slot, i] for i in range(pps)], axis=0)
    suf = jnp.zeros((pps * heads, 2 * PAGE_SIZE), F32)
    for piece in _split(lf):
        suf = suf + _dot(piece, wsuf_ref[...])
    running = carry_sc[...]
    for i in reversed(range(pps)):
        rsl = slice(i * heads, (i + 1) * heads)
        g_sc[:, i * PAGE_SIZE:(i + 1) * PAGE_SIZE] = (suf[rsl, :PAGE_SIZE] + running) * LOG2E
        running = running + suf[rsl, PAGE_SIZE:]
    carry_sc[...] = running

    per = pps // DEC_STREAMS
    width = per * PAGE_SIZE
    for st in range(DEC_STREAMS):
        pages = range(st * per, (st + 1) * per)
        kt = jnp.concatenate([k_buf[slot, i] for i in pages], axis=1).astype(BF16)
        vt = jnp.concatenate([v_buf[slot, i] for i in pages], axis=1).astype(BF16)
        s = _dot(q, kt)
        for h in range(heads):
            sl = slice(h * t_new, (h + 1) * t_new)
            s_sc[st, sl, :] = s[sl] + g_sc[h:h + 1, st * width:(st + 1) * width] + fn[sl, :1] * LOG2E
        _softmax_update(s_sc[st], vt, m_sc, l_sc, acc_sc, st, v_transposed=True)

    @pl.when(c == pl.num_programs(1) - 1)
    def _():
        rid = lax.broadcasted_iota(jnp.int32, (rows, LANES), 0) % t_new
        cid = lax.broadcasted_iota(jnp.int32, (rows, LANES), 1)
        diag = jnp.where(rid == cid, fn, 0.0)
        fcol = jnp.zeros((rows, LANES), F32)
        for piece in _split(diag):
            fcol = fcol + _dot(ones_blk_ref[...], piece)
        sn = _dot(q, kn_ref[...]) + (fn - fcol) * LOG2E
        sn = jnp.where(cid <= rid, sn, NEG_INF)
        _softmax_update(sn, vn_ref[...], m_sc, l_sc, acc_sc, 0, v_transposed=True)
        o_ref[...] = _softmax_merge(m_sc, l_sc, acc_sc)


def _fox_dec(page_table, q_bd, lf_rows, k_new_t, v_new_t, cache_kt, cache_vt, cache_lft, layer, heads, t_new):
    n, n_pages = page_table.shape
    pps = PAGES_PER_STEP
    assert n_pages % pps == 0 and pps % DEC_STREAMS == 0
    nc = n_pages // pps
    rows = heads * t_new
    width = cache_kt.shape[2]
    chunk_keys = pps * PAGE_SIZE
    assert rows == LANES

    jj = lax.broadcasted_iota(jnp.int32, (PAGE_SIZE, 2 * PAGE_SIZE), 0)
    ss = lax.broadcasted_iota(jnp.int32, (PAGE_SIZE, 2 * PAGE_SIZE), 1)
    wsuf = ((jj > ss) | (ss >= PAGE_SIZE)).astype(BF16)
    ra = lax.broadcasted_iota(jnp.int32, (rows, rows), 0)
    rb = lax.broadcasted_iota(jnp.int32, (rows, rows), 1)
    same_head = (ra // t_new) == (rb // t_new)
    tblk = (same_head & (rb % t_new <= ra % t_new)).astype(BF16)
    ones_blk = same_head.astype(BF16)

    const = lambda shape: pl.BlockSpec(shape, lambda s, c, pt: (0,) * len(shape))
    per_seq = lambda a: pl.BlockSpec((None,) + a.shape[1:], lambda s, c, pt: (s, 0, 0))
    hbm = pl.BlockSpec(memory_space=pl.ANY)
    in_specs = [per_seq(q_bd), per_seq(lf_rows), per_seq(k_new_t), per_seq(v_new_t),
                const(wsuf.shape), const(tblk.shape), const(ones_blk.shape), hbm, hbm, hbm]
    page_buf = lambda arr: pltpu.VMEM((2, pps) + arr.shape[2:], F32)
    grid_spec = pltpu.PrefetchScalarGridSpec(
        num_scalar_prefetch=1, grid=(n, nc), in_specs=in_specs,
        out_specs=pl.BlockSpec((None, rows, width), lambda s, c, pt: (s, 0, 0)),
        scratch_shapes=[page_buf(cache_kt), page_buf(cache_vt), page_buf(cache_lft), pltpu.SemaphoreType.DMA((3, 2)),
                        pltpu.VMEM((DEC_STREAMS, rows, 1), F32), pltpu.VMEM((DEC_STREAMS, rows, 1), F32),
                        pltpu.VMEM((DEC_STREAMS, rows, width), F32),
                        pltpu.VMEM((heads, PAGE_SIZE), F32), pltpu.VMEM((heads, chunk_keys), F32),
                        pltpu.VMEM((DEC_STREAMS, rows, chunk_keys // DEC_STREAMS), F32),
                        pltpu.VMEM((rows, LANES), F32)])
    return pl.pallas_call(
        functools.partial(_fox_dec_body, layer=layer, heads=heads, t_new=t_new, n_pages=n_pages),
        grid_spec=grid_spec,
        out_shape=jax.ShapeDtypeStruct((n, rows, width), F32),
        compiler_params=_params(2),
        name="fox_dec",
    )(page_table.reshape(-1), q_bd, lf_rows, k_new_t, v_new_t, wsuf, tblk, ones_blk, cache_kt, cache_vt, cache_lft)


def _conv_glu_body(x_ref, gmix_ref, w_ref, b_ref, u_out):
    h = _rms(x_ref[...], gmix_ref[...]).astype(BF16)
    ch = u_out.shape[1]
    step = 2 * LANES
    for c in range(ch // step):
        a = _dot(h, w_ref[:, c * step:(c + 1) * step]) + b_ref[:, c * step:(c + 1) * step]
        gate = _dot(h, w_ref[:, ch + c * step:ch + (c + 1) * step]) + b_ref[:, ch + c * step:ch + (c + 1) * step]
        u_out[:, c * step:(c + 1) * step] = a * jax.nn.sigmoid(gate)


def _conv_glu(x, gmix, w, b):
    r, d = x.shape
    ch = w.shape[1] // 2
    tm = _tile(r, (512, 256, 128, 64, 32, 16, 8))
    return pl.pallas_call(
        _conv_glu_body,
        grid=(r // tm,),
        in_specs=[pl.BlockSpec((tm, d), lambda i: (i, 0)), _full((1, d)), _full(w.shape), _full((1, 2 * ch))],
        out_specs=pl.BlockSpec((tm, ch), lambda i: (i, 0)),
        out_shape=jax.ShapeDtypeStruct((r, ch), F32),
        compiler_params=_params(1),
        name="conv_glu",
    )(x, gmix, w.astype(BF16), b.reshape(1, -1))


def _ln_silu(z, g, b):
    mu = jnp.mean(z, axis=-1, keepdims=True)
    zc = z - mu
    y = zc * lax.rsqrt(jnp.mean(zc * zc, axis=-1, keepdims=True) + NORM_EPS) * g + b
    return y * jax.nn.sigmoid(y)


HALO = 32


def _conv_dw_prompt_body(u_ref, halo_ref, w_ref, bdw_ref, g_ref, b_ref, z_out, ext_sc, *, width, tiles_per_seq):
    i = pl.program_id(0)
    tm = u_ref.shape[0]
    first = i % tiles_per_seq == 0
    ext_sc[:HALO, :] = jnp.where(first, 0.0, halo_ref[...])
    ext_sc[HALO:, :] = u_ref[...]
    acc = jnp.zeros(u_ref.shape, F32) + bdw_ref[...]
    off = HALO - (width - 1)
    for w in range(width):
        acc = acc + ext_sc[off + w:off + w + tm, :] * w_ref[w:w + 1, :]
    z_out[...] = _ln_silu(acc, g_ref[...], b_ref[...]).astype(BF16)


def _conv_dw_prompt(u, r, w_dw, b_dw, g_ln, b_ln, seq_rows):
    ch = u.shape[1]
    width = w_dw.shape[0]
    assert width - 1 <= HALO
    tm = _tile(seq_rows, (256, 128, 64, 32))
    ratio = tm // HALO
    vec = lambda a: a.reshape(1, ch)
    return pl.pallas_call(
        functools.partial(_conv_dw_prompt_body, width=width, tiles_per_seq=seq_rows // tm),
        grid=(r // tm,),
        in_specs=[pl.BlockSpec((tm, ch), lambda i: (i, 0)),
                  pl.BlockSpec((HALO, ch), lambda i: (jnp.maximum(i * ratio - 1, 0), 0)),
                  _full((width, ch)), _full((1, ch)), _full((1, ch)), _full((1, ch))],
        out_specs=pl.BlockSpec((tm, ch), lambda i: (i, 0)),
        out_shape=jax.ShapeDtypeStruct((r, ch), BF16),
        scratch_shapes=[pltpu.VMEM((tm + HALO, ch), F32)],
        compiler_params=_params(1),
        name="conv_dw_prompt",
    )(u, u, w_dw, vec(b_dw), vec(g_ln), vec(b_ln))


def _conv_dw_sample_body(ext_ref, w_ref, bdw_ref, g_ref, b_ref, z_out, *, width, t_new):
    acc = jnp.zeros(z_out.shape, F32) + bdw_ref[...]
    for w in range(width):
        acc = acc + ext_ref[:, w:w + t_new, :] * w_ref[w:w + 1, :]
    z_out[...] = _ln_silu(acc, g_ref[...], b_ref[...]).astype(BF16)


def _conv_dw_sample(ext, w_dw, b_dw, g_ln, b_ln, t_new):
    n, rows, ch = ext.shape
    width = w_dw.shape[0]
    nb = _tile(n, (16, 8, 4, 2, 1))
    vec = lambda a: a.reshape(1, ch)
    return pl.pallas_call(
        functools.partial(_conv_dw_sample_body, width=width, t_new=t_new),
        grid=(n // nb,),
        in_specs=[pl.BlockSpec((nb, rows, ch), lambda i: (i, 0, 0)),
                  _full((width, ch)), _full((1, ch)), _full((1, ch)), _full((1, ch))],
        out_specs=pl.BlockSpec((nb, t_new, ch), lambda i: (i, 0, 0)),
        out_shape=jax.ShapeDtypeStruct((n, t_new, ch), BF16),
        compiler_params=_params(1),
        name="conv_dw_sample",
    )(ext, w_dw, vec(b_dw), vec(g_ln), vec(b_ln))


def _pad_last(a, n):
    return jnp.pad(a, [(0, 0)] * (a.ndim - 1) + [(0, n - a.shape[-1])])


def _rope_tables(pos, half):
    inv = ROPE_BASE ** (-jnp.arange(half, dtype=F32) / half)
    ang = pos.astype(F32)[:, None] * inv[None, :]
    return jnp.cos(ang), jnp.sin(ang)


def _mla_layer(x, gmix, pos_rows, dims, cache_lat, cache_kpe, layer, page_table,
               w_dq, g_qa, w_uq, g_qn, w_dkv, g_kva, w_ukv, g_kn, w_o):
    batch, seq, n, t_new = dims
    d = x.shape[1]
    heads, qk_dim = w_uq.shape[1], w_uq.shape[2]
    kv_lora = w_ukv.shape[0]
    nope = w_ukv.shape[2] - w_o.shape[1]
    v_dim = w_o.shape[1]
    rope_dim = qk_dim - nope
    half = rope_dim // 2
    assert nope + rope_dim <= LANES and nope == 64 and rope_dim == 32 and v_dim < LANES
    bs = batch * seq
    past = page_table.shape[1] * PAGE_SIZE

    kpe_slot_w = jnp.concatenate([jnp.zeros((d, nope), F32), w_dkv[:, kv_lora:], jnp.zeros((d, LANES - qk_dim), F32)],
                                 axis=1)
    w_dkv_pad = jnp.concatenate([w_dkv[:, :kv_lora], kpe_slot_w, _swap_rope_halves(kpe_slot_w, nope, half)], axis=1)
    gslot = lambda g: _pad_last(g.reshape(1, qk_dim), LANES)
    w_uq_slot = _pad_last(w_uq, LANES)
    w = dict(
        w_dq=w_dq.astype(BF16), g_qa=g_qa.reshape(1, -1),
        w_uq=w_uq_slot.reshape(w_uq.shape[0], heads * LANES).astype(BF16), g_q=gslot(g_qn),
        w_uq_sw=_swap_rope_halves(w_uq_slot, nope, half).reshape(w_uq.shape[0], heads * LANES).astype(BF16),
        g_q_sw=_swap_rope_halves(gslot(g_qn), nope, half), g_k_sw=_swap_rope_halves(gslot(g_kn), nope, half),
        w_dkv=w_dkv_pad.astype(BF16), g_kva=g_kva.reshape(1, -1),
        w_uk=_pad_last(w_ukv[:, :, :nope], LANES).reshape(kv_lora, heads * LANES).astype(BF16),
        w_uv=_pad_last(w_ukv[:, :, nope:], LANES).reshape(kv_lora, heads * LANES).astype(BF16), g_k=gslot(g_kn),
        v_one=_sum_lane_row(heads, v_dim))
    shift = _score_shift(g_qn, g_kn, qk_dim, qk_dim ** -0.5 * LOG2E)
    lane_hot = _sum_lane_row(1, qk_dim)
    w.update(q_one=lane_hot, k_shift=-shift * lane_hot)
    cos, sin = _rope_tables(pos_rows, half)
    r = x.shape[0]
    cos_slot = jnp.concatenate([jnp.ones((r, nope), F32), cos, cos, jnp.ones((r, LANES - qk_dim), F32)], axis=1)
    sin_slot = jnp.concatenate([jnp.zeros((r, nope), F32), -sin, sin, jnp.zeros((r, LANES - qk_dim), F32)], axis=1)
    q, k, v, lat, kpe_slot = _mla_pre(x, gmix.reshape(1, d), w, cos_slot, sin_slot, heads, qk_dim)
    kpe = kpe_slot[:, nope:qk_dim]

    o_p = _flash(q, k, v, shift, batch, seq, heads, 1, 4, v_dim)

    wukt = jnp.transpose(w_ukv[:, :, :nope], (1, 2, 0))
    wcat = jnp.zeros((heads, LANES, kv_lora + LANES), F32)
    wcat = wcat.at[:, :nope, :kv_lora].set(wukt)
    wcat = wcat.at[:, nope:qk_dim, kv_lora:kv_lora + rope_dim].set(jnp.eye(rope_dim, dtype=F32))
    g_dec = _pad_last(jnp.concatenate([g_kn[:nope], jnp.ones((rope_dim,), F32)]).reshape(1, qk_dim), LANES)
    qdec = _mla_qdec(q[bs:], g_dec, wcat.astype(BF16)).reshape(heads, n, t_new, kv_lora + LANES)
    cos_t, sin_t = (a.T for a in _rope_tables(jnp.arange(past), half))
    cos_n, sin_n = (a.T for a in _rope_tables(past + jnp.arange(PAGE_SIZE), half))
    new_rows = lambda a: jnp.pad(a.reshape(n, t_new, -1), ((0, 0), (0, PAGE_SIZE - t_new), (0, 0)))
    o_lat = _mla_dec(page_table, qdec, wukt.reshape(heads * nope, kv_lora).astype(BF16),
                     g_kn[nope:].reshape(rope_dim, 1), cos_t, sin_t, cos_n, sin_n,
                     new_rows(lat[bs:]).astype(BF16), jnp.transpose(new_rows(kpe[bs:]), (0, 2, 1)),
                     cache_lat, jnp.transpose(cache_kpe, (0, 1, 3, 2)), layer, heads, qk_dim, nope)
    w_uv_dec = _pad_last(jnp.transpose(w_ukv[:, :, nope:], (1, 0, 2)), LANES).astype(BF16)
    o_s = _mla_unabsorb(o_lat.reshape(n, heads, t_new, kv_lora), w_uv_dec)

    w_o_pad = jnp.pad(w_o, ((0, 0), (0, LANES - v_dim), (0, 0))).reshape(heads * LANES, d)
    return (o_p, o_s, w_o_pad, jnp.zeros((d,), F32)), lat, kpe


def _fox_layer(x, gmix, dims, cache_k, cache_v, cache_lf, layer, page_table,
               w_q, w_k, w_v, w_f, b_f, g_qn, g_kn, w_o):
    batch, seq, n, t_new = dims
    d = x.shape[1]
    heads, head_dim = w_q.shape[1], w_q.shape[2]
    kv_heads = w_k.shape[1]
    group = heads // kv_heads
    bs = batch * seq
    sp = FORGET_SPLIT
    base = head_dim + 1
    shift_lane = base + sp * (group + 1)
    assert shift_lane < LANES and heads <= LANES
    shift = _score_shift(g_qn, g_kn, head_dim, head_dim ** -0.5 * LOG2E)

    place_q = jnp.zeros((sp, LANES, heads * LANES), F32)
    place_k = jnp.zeros((sp, LANES, kv_heads * LANES), F32)
    const_q = jnp.zeros((1, heads * LANES), F32)
    const_k = jnp.zeros((1, kv_heads * LANES), F32)
    hh = jnp.arange(heads)
    for j in range(sp):
        place_q = place_q.at[j, hh, hh * LANES + base + sp * group + j].set(1.0)
        place_k = place_k.at[j, hh, (hh // group) * LANES + base + sp * (hh % group) + j].set(-1.0)
        const_q = const_q.at[0, hh * LANES + base + sp * (hh % group) + j].set(1.0)
        const_k = const_k.at[0, jnp.arange(kv_heads) * LANES + base + sp * group + j].set(1.0)
    const_q = const_q.at[0, hh * LANES + shift_lane].set(1.0)
    const_k = const_k.at[0, jnp.arange(kv_heads) * LANES + shift_lane].set(-shift)
    gslot = lambda g: _pad_last(g.reshape(1, head_dim), LANES)
    w = dict(w_q=_pad_last(w_q, LANES).reshape(d, heads * LANES).astype(BF16),
             w_k=_pad_last(w_k, LANES).reshape(d, kv_heads * LANES).astype(BF16),
             w_v=_pad_last(w_v, LANES).reshape(d, kv_heads * LANES).astype(BF16),
             w_f=_pad_last(w_f, LANES).astype(BF16), b_f=_pad_last(b_f.reshape(1, heads), LANES),
             g_q=gslot(g_qn), g_k=gslot(g_kn), place_q=place_q.astype(BF16), place_k=place_k.astype(BF16),
             const_q=const_q, const_k=const_k, v_one=_sum_lane_row(kv_heads, head_dim))
    q_aug, k_aug, v_b, k_f, v_f, lf = _fox_pre(x, gmix.reshape(1, d), w, seq, heads, kv_heads, head_dim)
    unslot = lambda a, nh: a.reshape(a.shape[0], nh, LANES)[:, :, :head_dim]
    k_out, v_out, lf_out = unslot(k_f, kv_heads), unslot(v_f, kv_heads), lf[:, :heads]

    o_p = _flash(q_aug, k_aug, v_b, shift, batch, seq, heads, group, group, head_dim)

    kvw = kv_heads * head_dim
    q_s = unslot(q_aug[bs:], heads).reshape(n, t_new, kv_heads, group, head_dim)
    q_bd = jnp.einsum("ntkgd,kj->nkgtjd", q_s.astype(F32), jnp.eye(kv_heads, dtype=F32))
    q_bd = q_bd.reshape(n, heads * t_new, kvw).astype(BF16)
    lf_rows = jnp.broadcast_to(jnp.transpose(lf_out[bs:].reshape(n, t_new, heads), (0, 2, 1))[..., None],
                               (n, heads, t_new, LANES)).reshape(n, heads * t_new, LANES)
    new_t = lambda a: jnp.transpose(jnp.pad(a.reshape(n, t_new, kvw), ((0, 0), (0, PAGE_SIZE - t_new), (0, 0))),
                                    (0, 2, 1)).astype(BF16)
    pool = cache_k.shape[1]
    page_t = lambda c: jnp.transpose(c, (0, 1, 3, 4, 2)).reshape(c.shape[0], pool, kvw, PAGE_SIZE)
    o_bd = _fox_dec(page_table, q_bd, lf_rows, new_t(k_out[bs:]), new_t(v_out[bs:]), page_t(cache_k), page_t(cache_v),
                    jnp.transpose(cache_lf, (0, 1, 3, 2)), layer, heads, t_new)
    o_s = jnp.einsum("nkgtjd,kj->ntkgd", o_bd.reshape(n, kv_heads, group, t_new, kv_heads, head_dim),
                     jnp.eye(kv_heads, dtype=F32))
    o_s = _pad_last(o_s.reshape(n * t_new, heads, head_dim), LANES).reshape(n * t_new, heads * LANES).astype(BF16)

    w_o_pad = jnp.pad(w_o, ((0, 0), (0, LANES - head_dim), (0, 0))).reshape(heads * LANES, d)
    return (o_p, o_s, w_o_pad, jnp.zeros((d,), F32)), k_out, v_out, lf_out


def _conv_layer(x, gmix, dims, state, w_pw1, b_pw1, w_dw, b_dw, g_ln, b_ln, w_pw2, b_pw2):
    batch, seq, n, t_new = dims
    d = x.shape[1]
    bs = batch * seq
    keep = w_dw.shape[0] - 1
    assert seq >= keep
    u = _conv_glu(x, gmix.reshape(1, d), w_pw1, b_pw1)
    ch = u.shape[1]
    z_p = _conv_dw_prompt(u, bs, w_dw, b_dw, g_ln, b_ln, seq)
    ext_s = jnp.concatenate([state.astype(F32), u[bs:].reshape(n, t_new, ch)], axis=1)
    rows = -(-(keep + t_new) // 8) * 8
    z_s = _conv_dw_sample(jnp.pad(ext_s, ((0, 0), (0, rows - keep - t_new), (0, 0))), w_dw, b_dw, g_ln, b_ln, t_new)
    return (z_p, z_s.reshape(n * t_new, ch), w_pw2, b_pw2), u[:bs].reshape(batch, seq, ch)[:, seq - keep:], ext_s[:, t_new:]


def kernel(x_prompt, x_sample, cache_mla_latent, cache_mla_krope, cache_fox_k, cache_fox_v, cache_fox_logf, state_conv, page_table, norm_mix, norm_mlp, mlp_up, mlp_down, mla_w_dq, mla_g_qa, mla_w_uq, mla_g_qn, mla_w_dkv, mla_g_kva, mla_w_ukv, mla_g_kn, mla_w_o, fox_w_q, fox_w_k, fox_w_v, fox_w_f, fox_b_f, fox_g_qn, fox_g_kn, fox_w_o, conv_w_pw1, conv_b_pw1, conv_w_dw, conv_b_dw, conv_g_ln, conv_b_ln, conv_w_pw2, conv_b_pw2):
    batch, seq, d = x_prompt.shape
    n, t_new, _ = x_sample.shape
    dims = (batch, seq, n, t_new)
    bs = batch * seq
    past = page_table.shape[1] * PAGE_SIZE
    depth = norm_mix.shape[0]
    x = jnp.concatenate([x_prompt.reshape(bs, d), x_sample.reshape(n * t_new, d)], axis=0)
    pos_rows = jnp.concatenate([jnp.tile(jnp.arange(seq), batch), jnp.tile(past + jnp.arange(t_new), n)])

    outs = {name: [] for name in ("lat", "kpe", "fk", "fv", "flf", "cp", "cs")}
    for i in range(depth):
        kind, j = i % 3, i // 3
        if kind == 0:
            mix, lat, kpe = _mla_layer(x, norm_mix[i], pos_rows, dims, cache_mla_latent, cache_mla_krope, j, page_table,
                                       mla_w_dq[j], mla_g_qa[j], mla_w_uq[j], mla_g_qn[j], mla_w_dkv[j], mla_g_kva[j],
                                       mla_w_ukv[j], mla_g_kn[j], mla_w_o[j])
            outs["lat"].append(lat)
            outs["kpe"].append(kpe)
        elif kind == 1:
            mix, fk, fv, flf = _fox_layer(x, norm_mix[i], dims, cache_fox_k, cache_fox_v, cache_fox_logf, j, page_table,
                                          fox_w_q[j], fox_w_k[j], fox_w_v[j], fox_w_f[j], fox_b_f[j], fox_g_qn[j],
                                          fox_g_kn[j], fox_w_o[j])
            outs["fk"].append(fk)
            outs["fv"].append(fv)
            outs["flf"].append(flf)
        else:
            mix, cp, cs = _conv_layer(x, norm_mix[i], dims, state_conv[j], conv_w_pw1[j], conv_b_pw1[j], conv_w_dw[j],
                                      conv_b_dw[j], conv_g_ln[j], conv_b_ln[j], conv_w_pw2[j], conv_b_pw2[j])
            outs["cp"].append(cp)
            outs["cs"].append(cs)
        x = _mlp(x, *mix, norm_mlp[i], mlp_up[i], mlp_down[i])

    def split_rows(name):
        stacked = jnp.stack(outs[name])
        tail = stacked.shape[2:]
        return (stacked[:, :bs].reshape((-1, batch, seq) + tail), stacked[:, bs:].reshape((-1, n, t_new) + tail))

    lat_p, lat_s = split_rows("lat")
    kpe_p, kpe_s = split_rows("kpe")
    fk_p, fk_s = split_rows("fk")
    fv_p, fv_s = split_rows("fv")
    flf_p, flf_s = split_rows("flf")
    return (x[:bs].reshape(batch, seq, d), x[bs:].reshape(n, t_new, d), lat_p, lat_s, kpe_p, kpe_s,
            fk_p, fk_s, fv_p, fv_s, flf_p, flf_s, jnp.stack(outs["cp"]), jnp.stack(outs["cs"]))
```
